```python
import math
import jax, jax.numpy as jnp
from jax import lax
import numpy as np

D_MODEL = 2048
BATCH = 2
SEQ = 8192
DEPTH = 1
DEC_BATCH = 128
DEC_SEQ = 8
PAST_LEN = 16384
PAGE_SIZE = 128

A_GROUPS = ((128, 1), (512, 4), (2048, 16))
A_MAX_WINDOW = 2048
A_KV_HEADS = 4
A_Q_HEADS = A_KV_HEADS * len(A_GROUPS)
HEAD_DIM_A = 128
B_WINDOW = 128
B_Q_HEADS = 16
B_KV_HEADS = 2
B_GROUP = B_Q_HEADS // B_KV_HEADS
HEAD_DIM_B = 64
C_HEADS = 4
HEAD_DIM_C = 128
N_MEM = 256
BLK = 128
ROPE_THETA = 10000.0
RMS_EPS = 1e-6
N_BRANCH = 3
W_A = A_KV_HEADS * HEAD_DIM_A
W_B = B_Q_HEADS * HEAD_DIM_B
W_C = C_HEADS * HEAD_DIM_C
IN_SIZES = (A_Q_HEADS * HEAD_DIM_A, W_A, W_A,
            W_B, B_KV_HEADS * HEAD_DIM_B, B_KV_HEADS * HEAD_DIM_B,
            W_C, N_BRANCH * D_MODEL)
IN_WIDTH = sum(IN_SIZES)
PEER_HEADS = 8
PEER_N_KEYS = 128
PEER_N_EXPERTS = PEER_N_KEYS * PEER_N_KEYS
PEER_KEY_DIM = 256
PEER_TOPK = 16
PEER_TOK_BLOCK = 128

kernel_name = "hybrid_dilated_swa_sink_mem_peer_step"


def rmsnorm(x, g):
    xf = x.astype(jnp.float32)
    y = xf * lax.rsqrt(jnp.mean(xf * xf, axis=-1, keepdims=True) + RMS_EPS)
    return (y * g.astype(jnp.float32)).astype(x.dtype)


def rope(x, pos):
    half = x.shape[-1] // 2
    inv = ROPE_THETA ** (-jnp.arange(half, dtype=jnp.float32) / half)
    ang = pos.astype(jnp.float32)[:, None] * inv[None, :]
    cos = jnp.cos(ang)[None, :, None, :]
    sin = jnp.sin(ang)[None, :, None, :]
    xf = x.astype(jnp.float32)
    x1, x2 = xf[..., :half], xf[..., half:]
    return jnp.concatenate([x1 * cos - x2 * sin, x2 * cos + x1 * sin], axis=-1).astype(x.dtype)


def softmax_av(s, valid, v, sink, eq):
    s = jnp.where(valid, s, -jnp.inf)
    m = jnp.max(s, axis=-1)
    if sink is not None:
        m = jnp.maximum(m, sink)
    p = jnp.exp(s - m[..., None])
    den = jnp.sum(p, axis=-1)
    if sink is not None:
        den = den + jnp.exp(sink - m)
    o = jnp.einsum(eq, p.astype(v.dtype), v, preferred_element_type=jnp.float32) / den[..., None]
    return o, m + jnp.log(den)


def banded_attention(q, k, v, window, sink=None):
    N, S, KV, G, hd = q.shape
    nb = S // BLK
    qb = q.reshape(N, nb, BLK, KV, G, hd)

    def with_prev(t):
        tb = t.reshape(N, nb, BLK, KV, hd)
        prev = jnp.pad(tb, ((0, 0), (1, 0), (0, 0), (0, 0), (0, 0)))[:, :nb]
        return jnp.concatenate([prev, tb], axis=2)

    kk, vv = with_prev(k), with_prev(v)
    s = jnp.einsum('nbqkgd,nbckd->nbkgqc', qb, kk, preferred_element_type=jnp.float32) * hd ** -0.5
    qi = jnp.arange(BLK)[:, None] + BLK
    ci = jnp.arange(2 * BLK)[None, :]
    band = (qi - ci >= 0) & (qi - ci <= window)
    real = (jnp.arange(nb)[:, None, None] > 0) | (ci[None] >= BLK)
    valid = (band[None] & real)[None, :, None, None]
    o, lse = softmax_av(s, valid, vv, sink, 'nbkgqc,nbckd->nbkgqd')
    o = jnp.transpose(o, (0, 1, 4, 2, 3, 5)).reshape(N, S, KV, G, hd)
    lse = jnp.transpose(lse, (0, 1, 4, 2, 3)).reshape(N, S, KV, G)
    return o, lse


def gathered_attention(q, kg, vg, valid, sink=None):
    hd = q.shape[-1]
    s = jnp.einsum('ntkgd,ntjkd->ntkgj', q, kg, preferred_element_type=jnp.float32) * hd ** -0.5
    return softmax_av(s, valid[None, :, None, None, :], vg, sink, 'ntkgj,ntjkd->ntkgd')


def combine_groups(outs, lses):
    w = jax.nn.softmax(jnp.stack(lses, axis=0), axis=0)
    return jnp.sum(w[..., None] * jnp.stack(outs, axis=0), axis=0)


def to_res(t, d, sp):
    B, S = t.shape[:2]
    t = jnp.pad(t, ((0, 0), (0, sp - S)) + ((0, 0),) * (t.ndim - 2))
    t = t.reshape((B, sp // d, d) + t.shape[2:])
    return jnp.swapaxes(t, 1, 2).reshape((B * d, sp // d) + t.shape[3:])


def from_res(t, B, d, S):
    u = t.shape[1]
    t = t.reshape((B, d, u) + t.shape[2:])
    return jnp.swapaxes(t, 1, 2).reshape((B, d * u) + t.shape[3:])[:, :S]


def dilated_prompt(q, k, v):
    B, S = q.shape[:2]
    qg = q.reshape(B, S, len(A_GROUPS), A_KV_HEADS, 1, HEAD_DIM_A)
    outs, lses = [], []
    for gi, (w, d) in enumerate(A_GROUPS):
        sp = -(-S // (d * BLK)) * d * BLK
        o, lse = banded_attention(to_res(qg[:, :, gi], d, sp), to_res(k, d, sp), to_res(v, d, sp), w // d)
        outs.append(from_res(o[:, :, :, 0], B, d, S))
        lses.append(from_res(lse[..., 0], B, d, S))
    return combine_groups(outs, lses)


def dilated_sample(q, k_new, v_new, k_buf, v_buf):
    N, T = q.shape[:2]
    L = k_buf.shape[1]
    kc = jnp.concatenate([k_buf, k_new.astype(k_buf.dtype)], axis=1)
    vc = jnp.concatenate([v_buf, v_new.astype(v_buf.dtype)], axis=1)
    qg = q.reshape(N, T, len(A_GROUPS), A_KV_HEADS, 1, HEAD_DIM_A)
    outs, lses = [], []
    for gi, (w, d) in enumerate(A_GROUPS):
        idx = L + jnp.arange(T)[:, None] - d * jnp.arange(w // d + 1)[None, :]
        valid = idx >= 0
        idx = jnp.maximum(idx, 0)
        o, lse = gathered_attention(qg[:, :, gi], kc[:, idx], vc[:, idx], valid)
        outs.append(o[:, :, :, 0])
        lses.append(lse[..., 0])
    return combine_groups(outs, lses)


def window_prompt(q, k, v, sink):
    N, S = q.shape[:2]
    qg = q.reshape(N, S, B_KV_HEADS, B_GROUP, HEAD_DIM_B)
    o, _ = banded_attention(qg, k, v, B_WINDOW,
                            sink.astype(jnp.float32).reshape(B_KV_HEADS, B_GROUP, 1))
    return o.reshape(N, S, W_B)


def window_sample(q, k_new, v_new, k_buf, v_buf, sink):
    N, T = q.shape[:2]
    L = k_buf.shape[1]
    kc = jnp.concatenate([k_buf, k_new.astype(k_buf.dtype)], axis=1)
    vc = jnp.concatenate([v_buf, v_new.astype(v_buf.dtype)], axis=1)
    idx = L + jnp.arange(T)[:, None] - jnp.arange(B_WINDOW + 1)[None, :]
    valid = idx >= 0
    idx = jnp.maximum(idx, 0)
    qg = q.reshape(N, T, B_KV_HEADS, B_GROUP, HEAD_DIM_B)
    o, _ = gathered_attention(qg, kc[:, idx], vc[:, idx], valid,
                              sink.astype(jnp.float32).reshape(B_KV_HEADS, B_GROUP))
    return o.reshape(N, T, W_B)


def memory_kv(mem, g_mem, w_mem_kv):
    N, M = mem.shape[:2]
    kv = jnp.einsum('nmd,de->nme', rmsnorm(mem, g_mem), w_mem_kv)
    return (kv[..., :W_C].reshape(N, M, C_HEADS, HEAD_DIM_C),
            kv[..., W_C:].reshape(N, M, C_HEADS, HEAD_DIM_C))


def memory_attention(q, mk, mv):
    s = jnp.einsum('nthd,nmhd->nhtm', q, mk, preferred_element_type=jnp.float32) * HEAD_DIM_C ** -0.5
    p = jax.nn.softmax(s, axis=-1)
    return jnp.einsum('nhtm,nmhd->nthd', p.astype(mv.dtype), mv, preferred_element_type=jnp.float32)


def mixer_inputs(x, pos, g_attn, w_in):
    N, T = x.shape[:2]
    z = jnp.einsum('ntd,de->nte', rmsnorm(x, g_attn), w_in)
    parts, off = [], 0
    for n in IN_SIZES:
        parts.append(z[..., off:off + n])
        off += n
    qa, ka, va, qb, kb, vb, qc, gates = parts
    qa = rope(qa.reshape(N, T, A_Q_HEADS, HEAD_DIM_A), pos)
    ka = rope(ka.reshape(N, T, A_KV_HEADS, HEAD_DIM_A), pos)
    va = va.reshape(N, T, A_KV_HEADS, HEAD_DIM_A)
    qb = rope(qb.reshape(N, T, B_Q_HEADS, HEAD_DIM_B), pos)
    kb = rope(kb.reshape(N, T, B_KV_HEADS, HEAD_DIM_B), pos)
    vb = vb.reshape(N, T, B_KV_HEADS, HEAD_DIM_B)
    qc = qc.reshape(N, T, C_HEADS, HEAD_DIM_C)
    return qa, ka, va, qb, kb, vb, qc, gates


def peer(h, w_q, k1, k2, u, v):
    N, T, D = h.shape
    n = N * T
    npad = -(-n // PEER_TOK_BLOCK) * PEER_TOK_BLOCK
    xb = jnp.pad(h.reshape(n, D), ((0, npad - n), (0, 0))).reshape(npad // PEER_TOK_BLOCK, PEER_TOK_BLOCK, D)
    half = PEER_KEY_DIM // 2

    def block(xt):
        q = jnp.einsum('td,de->te', xt, w_q).reshape(-1, PEER_HEADS, 2, half)
        s1 = jnp.einsum('thc,kc->thk', q[:, :, 0], k1, preferred_element_type=jnp.float32)
        s2 = jnp.einsum('thc,kc->thk', q[:, :, 1], k2, preferred_element_type=jnp.float32)
        v1, i1 = lax.top_k(s1, PEER_TOPK)
        v2, i2 = lax.top_k(s2, PEER_TOPK)
        cand = (v1[..., :, None] + v2[..., None, :]).reshape(-1, PEER_HEADS, PEER_TOPK * PEER_TOPK)
        cidx = (i1[..., :, None] * PEER_N_KEYS + i2[..., None, :]).reshape(-1, PEER_HEADS, PEER_TOPK * PEER_TOPK)
        top, sel = lax.top_k(cand, PEER_TOPK)
        eidx = jnp.take_along_axis(cidx, sel, axis=-1)
        g = jax.nn.softmax(top, axis=-1)
        ue, ve = u[eidx], v[eidx]
        a = jax.nn.gelu(jnp.einsum('td,thkd->thk', xt, ue, preferred_element_type=jnp.float32),
                        approximate=False)
        return jnp.einsum('thk,thkd->td', (g * a).astype(ve.dtype), ve)

    out = lax.map(block, xb)
    return out.reshape(npad, D)[:n].reshape(N, T, D).astype(h.dtype)


def finish_layer(x, gates, o_a, o_b, o_c, w_br_a, w_br_b, w_br_c, w_o, g_ffn,
                 w_peer_q, peer_k1, peer_k2, peer_u, peer_v):
    N, T, D = x.shape
    g = jax.nn.sigmoid(gates.astype(jnp.float32)).reshape(N, T, N_BRANCH, D)
    pa = jnp.einsum('nte,ed->ntd', o_a.reshape(N, T, W_A).astype(x.dtype), w_br_a, preferred_element_type=jnp.float32)
    pb = jnp.einsum('nte,ed->ntd', o_b.reshape(N, T, W_B).astype(x.dtype), w_br_b, preferred_element_type=jnp.float32)
    pc = jnp.einsum('nte,ed->ntd', o_c.reshape(N, T, W_C).astype(x.dtype), w_br_c, preferred_element_type=jnp.float32)
    merged = (g[:, :, 0] * pa + g[:, :, 1] * pb + g[:, :, 2] * pc).astype(x.dtype)
    x = x + jnp.einsum('ntd,de->nte', merged, w_o)
    return x + peer(rmsnorm(x, g_ffn), w_peer_q, peer_k1, peer_k2, peer_u, peer_v)


def setup_inputs(seed: int = 0) -> dict:
    key = jax.random.key(seed)
    ks = iter(jax.random.split(key, 32))

    def nrm(shape, scale):
        return jax.random.normal(next(ks), shape, jnp.float32) * scale

    la = min(A_MAX_WINDOW, PAST_LEN)
    lb = min(B_WINDOW, PAST_LEN)
    return {
        "x_prompt": nrm((BATCH, SEQ, D_MODEL), 1.0),
        "x_sample": nrm((DEC_BATCH, DEC_SEQ, D_MODEL), 1.0),
        "cache_a_k": nrm((DEPTH, DEC_BATCH, la, A_KV_HEADS, HEAD_DIM_A), 1.0),
        "cache_a_v": nrm((DEPTH, DEC_BATCH, la, A_KV_HEADS, HEAD_DIM_A), 1.0),
        "cache_b_k": nrm((DEPTH, DEC_BATCH, lb, B_KV_HEADS, HEAD_DIM_B), 1.0),
        "cache_b_v": nrm((DEPTH, DEC_BATCH, lb, B_KV_HEADS, HEAD_DIM_B), 1.0),
        "cache_mem_k": nrm((DEPTH, DEC_BATCH, N_MEM, C_HEADS, HEAD_DIM_C), 1.0),
        "cache_mem_v": nrm((DEPTH, DEC_BATCH, N_MEM, C_HEADS, HEAD_DIM_C), 1.0),
        "mem_prompt": nrm((BATCH, N_MEM, D_MODEL), 1.0),
        "g_attn": 1.0 + nrm((DEPTH, D_MODEL), 0.02),
        "w_in": nrm((DEPTH, D_MODEL, IN_WIDTH), D_MODEL ** -0.5),
        "sink_b": nrm((DEPTH, B_Q_HEADS), 1.0),
        "w_br_a": nrm((DEPTH, W_A, D_MODEL), W_A ** -0.5),
        "w_br_b": nrm((DEPTH, W_B, D_MODEL), W_B ** -0.5),
        "w_br_c": nrm((DEPTH, W_C, D_MODEL), W_C ** -0.5),
        "w_o": nrm((DEPTH, D_MODEL, D_MODEL), D_MODEL ** -0.5),
        "g_mem": 1.0 + nrm((DEPTH, D_MODEL), 0.02),
        "w_mem_kv": nrm((DEPTH, D_MODEL, 2 * W_C), D_MODEL ** -0.5),
        "g_ffn": 1.0 + nrm((DEPTH, D_MODEL), 0.02),
        "w_peer_q": nrm((DEPTH, D_MODEL, PEER_HEADS * PEER_KEY_DIM), D_MODEL ** -0.5),
        "peer_k1": nrm((DEPTH, PEER_N_KEYS, PEER_KEY_DIM // 2), (PEER_KEY_DIM // 2) ** -0.5),
        "peer_k2": nrm((DEPTH, PEER_N_KEYS, PEER_KEY_DIM // 2), (PEER_KEY_DIM // 2) ** -0.5),
        "peer_u": nrm((DEPTH, PEER_N_EXPERTS, D_MODEL), D_MODEL ** -0.5),
        "peer_v": nrm((DEPTH, PEER_N_EXPERTS, D_MODEL), 0.5),
        "g_final": 1.0 + nrm((D_MODEL,), 0.02),
    }


def reference(x_prompt, x_sample, cache_a_k, cache_a_v, cache_b_k, cache_b_v, cache_mem_k, cache_mem_v,
              mem_prompt, g_attn, w_in, sink_b, w_br_a, w_br_b, w_br_c, w_o, g_mem, w_mem_kv, g_ffn,
              w_peer_q, peer_k1, peer_k2, peer_u, peer_v, g_final):
    S = x_prompt.shape[1]
    T = x_sample.shape[1]
    pos_p = jnp.arange(S, dtype=jnp.int32)
    pos_s = PAST_LEN + jnp.arange(T, dtype=jnp.int32)
    la_p = min(A_MAX_WINDOW, S)
    lb_p = min(B_WINDOW, S)
    hp, hs = x_prompt, x_sample
    ak_p, av_p, bk_p, bv_p, mk_p, mv_p = [], [], [], [], [], []
    ak_s, av_s, bk_s, bv_s = [], [], [], []
    for l in range(DEPTH):
        qa, ka, va, qb, kb, vb, qc, gates = mixer_inputs(hp, pos_p, g_attn[l], w_in[l])
        o_a = dilated_prompt(qa, ka, va)
        o_b = window_prompt(qb, kb, vb, sink_b[l])
        mk, mv = memory_kv(mem_prompt, g_mem[l], w_mem_kv[l])
        o_c = memory_attention(qc, mk, mv)
        hp = finish_layer(hp, gates, o_a, o_b, o_c, w_br_a[l], w_br_b[l], w_br_c[l], w_o[l], g_ffn[l],
                          w_peer_q[l], peer_k1[l], peer_k2[l], peer_u[l], peer_v[l])
        ak_p.append(ka[:, S - la_p:])
        av_p.append(va[:, S - la_p:])
        bk_p.append(kb[:, S - lb_p:])
        bv_p.append(vb[:, S - lb_p:])
        mk_p.append(mk)
        mv_p.append(mv)
        qa, ka, va, qb, kb, vb, qc, gates = mixer_inputs(hs, pos_s, g_attn[l], w_in[l])
        o_a = dilated_sample(qa, ka, va, cache_a_k[l], cache_a_v[l])
        o_b = window_sample(qb, kb, vb, cache_b_k[l], cache_b_v[l], sink_b[l])
        o_c = memory_attention(qc, cache_mem_k[l], cache_mem_v[l])
        hs = finish_layer(hs, gates, o_a, o_b, o_c, w_br_a[l], w_br_b[l], w_br_c[l], w_o[l], g_ffn[l],
                          w_peer_q[l], peer_k1[l], peer_k2[l], peer_u[l], peer_v[l])
        ak_s.append(ka)
        av_s.append(va)
        bk_s.append(kb)
        bv_s.append(vb)
    y_prompt = rmsnorm(hp, g_final)
    y_sample = rmsnorm(hs, g_final)
    return (y_prompt, y_sample,
            jnp.stack(ak_p), jnp.stack(av_p), jnp.stack(bk_p), jnp.stack(bv_p),
            jnp.stack(mk_p), jnp.stack(mv_p),
            jnp.stack(ak_s), jnp.stack(av_s), jnp.stack(bk_s), jnp.stack(bv_s))
```

```python
import functools
import math

import jax
import jax.numpy as jnp
from jax import lax
from jax.experimental import pallas as pl
from jax.experimental.pallas import tpu as pltpu

F32 = jnp.float32
BF16 = jnp.bfloat16

PAST_LEN = 16384
A_GROUPS = ((128, 1), (512, 4), (2048, 16))
A_KV_HEADS = 4
HEAD_DIM_A = 128
B_WINDOW = 128
B_Q_HEADS = 16
B_KV_HEADS = 2
HEAD_DIM_B = 64
C_HEADS = 4
HEAD_DIM_C = 128
BLK = 128
ROPE_THETA = 10000.0
RMS_EPS = 1e-6
PEER_HEADS = 8
PEER_N_KEYS = 128
PEER_TOPK = 16

LANES = 128
NEG = -1e30
VMEM_LIMIT = 56 * 1024 * 1024

W_A = A_KV_HEADS * HEAD_DIM_A
QA_W = len(A_GROUPS) * W_A
QB_W = B_Q_HEADS * HEAD_DIM_B
KVB_W = 2 * B_KV_HEADS * HEAD_DIM_B
QC_W = C_HEADS * HEAD_DIM_C
IN_TN = 256


def _cparams(*sem):
    return pltpu.CompilerParams(dimension_semantics=sem, vmem_limit_bytes=VMEM_LIMIT)


def _dot_nt(a, b):
    return lax.dot_general(a, b, (((1,), (1,)), ((), ())), preferred_element_type=F32)


def _dot_tn(a, b):
    return lax.dot_general(a, b, (((0,), (0,)), ((), ())), preferred_element_type=F32)


def _dot(a, b):
    return jnp.dot(a, b, preferred_element_type=F32)


def _rms_kernel(x_ref, g_ref, o_ref):
    x = x_ref[...]
    ms = jnp.mean(x * x, axis=-1, keepdims=True)
    o_ref[...] = (x * lax.rsqrt(ms + RMS_EPS) * g_ref[...]).astype(o_ref.dtype)


def _rmsnorm(x2d, g, out_dtype, tm=512):
    R, D = x2d.shape
    tm = min(tm, R)
    return pl.pallas_call(
        _rms_kernel,
        grid=(R // tm,),
        in_specs=[pl.BlockSpec((tm, D), lambda i: (i, 0)), pl.BlockSpec((1, D), lambda i: (0, 0))],
        out_specs=pl.BlockSpec((tm, D), lambda i: (i, 0)),
        out_shape=jax.ShapeDtypeStruct((R, D), out_dtype),
        compiler_params=_cparams("parallel"),
        name="rmsnorm",
    )(x2d, g.reshape(1, D))


def _add_rms_kernel(a_ref, b_ref, g_ref, o_ref):
    x = a_ref[...] + b_ref[...]
    ms = jnp.mean(x * x, axis=-1, keepdims=True)
    o_ref[...] = x * lax.rsqrt(ms + RMS_EPS) * g_ref[...]


def _add_rmsnorm(a, b, g, tm=512):
    R, D = a.shape
    tm = min(tm, R)
    return pl.pallas_call(
        _add_rms_kernel,
        grid=(R // tm,),
        in_specs=[pl.BlockSpec((tm, D), lambda i: (i, 0)), pl.BlockSpec((tm, D), lambda i: (i, 0)),
                  pl.BlockSpec((1, D), lambda i: (0, 0))],
        out_specs=pl.BlockSpec((tm, D), lambda i: (i, 0)),
        out_shape=jax.ShapeDtypeStruct((R, D), F32),
        compiler_params=_cparams("parallel"),
        name="add_rmsnorm",
    )(a, b, g.reshape(1, D))


def _mm_kernel(a_ref, b_ref, o_ref):
    o_ref[...] = _dot(a_ref[...], b_ref[...])


def _matmul_f32(a, b, tm, tn):
    M, K = a.shape
    N = b.shape[1]
    return pl.pallas_call(
        _mm_kernel,
        grid=(M // tm, N // tn),
        in_specs=[pl.BlockSpec((tm, K), lambda i, j: (i, 0)), pl.BlockSpec((K, tn), lambda i, j: (0, j))],
        out_specs=pl.BlockSpec((tm, tn), lambda i, j: (i, j)),
        out_shape=jax.ShapeDtypeStruct((M, N), F32),
        compiler_params=_cparams("parallel", "arbitrary"),
        name="matmul",
    )(a, b)


_T_QA = (0, QA_W // IN_TN)
_T_KA = (_T_QA[1], _T_QA[1] + W_A // IN_TN)
_T_VA = (_T_KA[1], _T_KA[1] + W_A // IN_TN)
_T_QB = (_T_VA[1], _T_VA[1] + QB_W // IN_TN)
_T_KVB = (_T_QB[1], _T_QB[1] + KVB_W // IN_TN)
_T_QC = (_T_KVB[1], _T_KVB[1] + QC_W // IN_TN)


def _in_kernel(h_ref, w_ref, ca_ref, sa_ref, cb_ref, sb_ref,
               qa_ref, ka_ref, va_ref, qb_ref, kvb_ref, qc_ref, gt_ref, z_ref):
    j = pl.program_id(1)
    z_ref[...] = _dot(h_ref[...], w_ref[...])
    halves = [slice(c * LANES, (c + 1) * LANES) for c in range(IN_TN // LANES)]

    def rope_a(cs):
        zh = z_ref[:, cs]
        return zh * ca_ref[...] + pltpu.roll(zh, HEAD_DIM_A // 2, 1) * sa_ref[...]

    def rope_b(cs):
        zh = z_ref[:, cs]
        lane = lax.broadcasted_iota(jnp.int32, zh.shape, 1)
        first = (lane & (HEAD_DIM_B // 2)) == 0
        partner = jnp.where(first, pltpu.roll(zh, LANES - HEAD_DIM_B // 2, 1),
                            pltpu.roll(zh, HEAD_DIM_B // 2, 1))
        return zh * cb_ref[...] + partner * sb_ref[...]

    @pl.when(j < _T_QA[1])
    def _():
        for cs in halves:
            qa_ref[:, cs] = (rope_a(cs) * (HEAD_DIM_A ** -0.5)).astype(BF16)

    @pl.when((j >= _T_KA[0]) & (j < _T_KA[1]))
    def _():
        for cs in halves:
            ka_ref[:, cs] = rope_a(cs)

    @pl.when((j >= _T_VA[0]) & (j < _T_VA[1]))
    def _():
        va_ref[...] = z_ref[...]

    @pl.when((j >= _T_QB[0]) & (j < _T_QB[1]))
    def _():
        for cs in halves:
            qb_ref[:, cs] = (rope_b(cs) * (HEAD_DIM_B ** -0.5)).astype(BF16)

    @pl.when(j == _T_KVB[0])
    def _():
        kvb_ref[:, halves[0]] = rope_b(halves[0])
        kvb_ref[:, halves[1]] = z_ref[:, halves[1]]

    @pl.when((j >= _T_QC[0]) & (j < _T_QC[1]))
    def _():
        qc_ref[...] = (z_ref[...] * (HEAD_DIM_C ** -0.5)).astype(BF16)

    @pl.when(j >= _T_QC[1])
    def _():
        gt_ref[...] = jax.nn.sigmoid(z_ref[...]).astype(BF16)


def _in_proj(h, w_bf, tabs, tm):
    R, D = h.shape
    n_w = w_bf.shape[1]
    n_tiles = n_w // IN_TN
    n_tab = tabs[0].shape[0] // tm
    gates_w = n_w - _T_QC[1] * IN_TN

    def seg(lo, hi):
        return pl.BlockSpec((tm, IN_TN), lambda i, j: (i, jnp.clip(j - lo, 0, hi - lo - 1)))

    tab = pl.BlockSpec((tm, LANES), lambda i, j: (i % n_tab, 0))
    outs = [
        (seg(*_T_QA), jax.ShapeDtypeStruct((R, QA_W), BF16)),
        (seg(*_T_KA), jax.ShapeDtypeStruct((R, W_A), F32)),
        (seg(*_T_VA), jax.ShapeDtypeStruct((R, W_A), F32)),
        (seg(*_T_QB), jax.ShapeDtypeStruct((R, QB_W), BF16)),
        (seg(*_T_KVB), jax.ShapeDtypeStruct((R, KVB_W), F32)),
        (seg(*_T_QC), jax.ShapeDtypeStruct((R, QC_W), BF16)),
        (seg(_T_QC[1], n_tiles), jax.ShapeDtypeStruct((R, gates_w), BF16)),
    ]
    return pl.pallas_call(
        _in_kernel,
        grid=(R // tm, n_tiles),
        in_specs=[pl.BlockSpec((tm, D), lambda i, j: (i, 0)),
                  pl.BlockSpec((D, IN_TN), lambda i, j: (0, j)), tab, tab, tab, tab],
        out_specs=[o[0] for o in outs],
        out_shape=[o[1] for o in outs],
        scratch_shapes=[pltpu.VMEM((tm, IN_TN), F32)],
        compiler_params=_cparams("parallel", "arbitrary"),
        name="in_proj",
    )(h, w_bf, *tabs)


def _rope_tables(pos):
    def tab(hd):
        half = hd // 2
        inv = ROPE_THETA ** (-jnp.arange(half, dtype=F32) / half)
        ang = pos.astype(F32)[:, None] * inv[None, :]
        cos, sin = jnp.cos(ang), jnp.sin(ang)
        rep = LANES // hd
        return (jnp.tile(jnp.concatenate([cos, cos], -1), (1, rep)),
                jnp.tile(jnp.concatenate([-sin, sin], -1), (1, rep)))
    ca, sa = tab(HEAD_DIM_A)
    cb, sb = tab(HEAD_DIM_B)
    return ca, sa, cb, sb


def _band_masks():
    r = lax.broadcasted_iota(jnp.int32, (BLK, BLK), 0)
    c = lax.broadcasted_iota(jnp.int32, (BLK, BLK), 1)
    return c <= r, c >= r


def _band_softmax(q, kown, vown, kprev, vprev, own_mask, prev_mask, sink=None):
    s_o = jnp.where(own_mask, _dot_nt(q, kown), NEG)
    s_p = jnp.where(prev_mask, _dot_nt(q, kprev), NEG)
    m = jnp.maximum(jnp.max(s_o, axis=-1, keepdims=True), jnp.max(s_p, axis=-1, keepdims=True))
    if sink is not None:
        m = jnp.maximum(m, sink)
    p_o = jnp.exp(s_o - m)
    p_p = jnp.exp(s_p - m)
    den = jnp.sum(p_o, axis=-1, keepdims=True) + jnp.sum(p_p, axis=-1, keepdims=True)
    if sink is not None:
        den = den + jnp.exp(sink - m)
    acc = _dot(p_o.astype(BF16), vown) + _dot(p_p.astype(BF16), vprev)
    return acc / den, m, den


def _attn_a_kernel(q_ref, kc_ref, kp_ref, vc_ref, vp_ref, o_ref, l_ref, *, nsub):
    first = pl.program_id(2) == 0
    own_mask, prev_mask = _band_masks()
    prev_mask0 = prev_mask & jnp.logical_not(first)
    for h in range(A_KV_HEADS):
        hs = slice(h * HEAD_DIM_A, (h + 1) * HEAD_DIM_A)
        kprev = kp_ref[0, :, hs].astype(BF16)
        vprev = vp_ref[0, :, hs].astype(BF16)
        for b in range(nsub):
            rs = slice(b * BLK, (b + 1) * BLK)
            kown = kc_ref[0, rs, hs].astype(BF16)
            vown = vc_ref[0, rs, hs].astype(BF16)
            o, m, den = _band_softmax(q_ref[0, rs, hs], kown, vown, kprev, vprev,
                                      own_mask, prev_mask0 if b == 0 else prev_mask)
            o_ref[0, rs, hs] = o
            l_ref[0, rs, hs] = jnp.broadcast_to(m + jnp.log(den), (BLK, HEAD_DIM_A))
            kprev, vprev = kown, vown


def _attn_a_group(qa, ka, va, B, S, g, d, tq=512):
    Sd = S // d
    tq = min(tq, Sd)
    nsub = tq // BLK
    q3 = qa.reshape(B, Sd, d * QA_W)
    k3 = ka.reshape(B, Sd, d * W_A)
    v3 = va.reshape(B, Sd, d * W_A)
    ng = len(A_GROUPS)
    cur = pl.BlockSpec((1, tq, W_A), lambda b, r, i: (b, i, r))
    prev = pl.BlockSpec((1, BLK, W_A), lambda b, r, i: (b, jnp.maximum(i * nsub - 1, 0), r))
    o, lse = pl.pallas_call(
        functools.partial(_attn_a_kernel, nsub=nsub),
        grid=(B, d, Sd // tq),
        in_specs=[pl.BlockSpec((1, tq, W_A), lambda b, r, i: (b, i, r * ng + g)), cur, prev, cur, prev],
        out_specs=[cur, cur],
        out_shape=[jax.ShapeDtypeStruct((B, Sd, d * W_A), F32)] * 2,
        compiler_params=_cparams("parallel", "parallel", "arbitrary"),
        name=f"attn_a_g{g}",
    )(q3, k3, k3, v3, v3)
    return o.reshape(B * S, W_A), lse.reshape(B * S, W_A)


def _dup_half(x, kv):
    lane = lax.broadcasted_iota(jnp.int32, x.shape, 1)
    lo = lane < HEAD_DIM_B
    rolled = pltpu.roll(x, HEAD_DIM_B, 1)
    return jnp.where(lo, x, rolled) if kv == 0 else jnp.where(lo, rolled, x)


def _attn_b_kernel(sink_ref, q_ref, kvc_ref, kvp_ref, o_ref, *, nsub):
    first = pl.program_id(1) == 0
    own_mask, prev_mask = _band_masks()
    prev_mask0 = prev_mask & jnp.logical_not(first)
    lo = lax.broadcasted_iota(jnp.int32, (BLK, LANES), 1) < HEAD_DIM_B
    pairs = B_Q_HEADS // B_KV_HEADS // 2
    for kv in range(B_KV_HEADS):
        kprev = _dup_half(kvp_ref[0, :, 0:LANES], kv).astype(BF16)
        vprev = _dup_half(kvp_ref[0, :, LANES:2 * LANES], kv).astype(BF16)
        for b in range(nsub):
            rs = slice(b * BLK, (b + 1) * BLK)
            kown = _dup_half(kvc_ref[0, rs, 0:LANES], kv).astype(BF16)
            vown = _dup_half(kvc_ref[0, rs, LANES:2 * LANES], kv).astype(BF16)
            for m in range(pairs):
                cs = slice((kv * pairs + m) * LANES, (kv * pairs + m + 1) * LANES)
                qp = q_ref[0, rs, cs]
                outs = []
                for par in range(2):
                    qh = jnp.where(lo if par == 0 else jnp.logical_not(lo), qp, jnp.zeros_like(qp))
                    sink = sink_ref[kv * 2 * pairs + 2 * m + par]
                    o, _, _ = _band_softmax(qh, kown, vown, kprev, vprev, own_mask,
                                            prev_mask0 if b == 0 else prev_mask, sink)
                    outs.append(o)
                o_ref[0, rs, cs] = jnp.where(lo, outs[0], outs[1]).astype(BF16)
            kprev, vprev = kown, vown


def _attn_b(sink, qb, kvb, B, S, tq=512):
    nsub = tq // BLK
    q3 = qb.reshape(B, S, QB_W)
    kv3 = kvb.reshape(B, S, KVB_W)
    o = pl.pallas_call(
        functools.partial(_attn_b_kernel, nsub=nsub),
        grid=(B, S // tq),
        in_specs=[pl.BlockSpec(memory_space=pltpu.SMEM),
                  pl.BlockSpec((1, tq, QB_W), lambda b, i: (b, i, 0)),
                  pl.BlockSpec((1, tq, KVB_W), lambda b, i: (b, i, 0)),
                  pl.BlockSpec((1, BLK, KVB_W), lambda b, i: (b, jnp.maximum(i * nsub - 1, 0), 0))],
        out_specs=pl.BlockSpec((1, tq, QB_W), lambda b, i: (b, i, 0)),
        out_shape=jax.ShapeDtypeStruct((B, S, QB_W), BF16),
        compiler_params=_cparams("parallel", "arbitrary"),
        name="attn_b",
    )(sink, q3, kv3, kv3)
    return o.reshape(B * S, QB_W)


def _xattn_kernel(q_ref, k_ref, v_ref, o_ref, *, nb):
    for n in range(nb):
        for h in range(C_HEADS):
            hs = slice(h * HEAD_DIM_C, (h + 1) * HEAD_DIM_C)
            q = q_ref[n, :, hs].astype(BF16)
            s = _dot_nt(q, k_ref[n, :, hs].astype(BF16))
            m = jnp.max(s, axis=-1, keepdims=True)
            p = jnp.exp(s - m)
            den = jnp.sum(p, axis=-1, keepdims=True)
            o = _dot(p.astype(BF16), v_ref[n, :, hs].astype(BF16)) / den
            o_ref[n, :, hs] = o.astype(o_ref.dtype)


def _xattn(q, mk, mv, nb, tq):
    N, Tq, W = q.shape
    M = mk.shape[1]
    kvspec = pl.BlockSpec((nb, M, W), lambda n, i: (n, 0, 0))
    return pl.pallas_call(
        functools.partial(_xattn_kernel, nb=nb),
        grid=(N // nb, Tq // tq),
        in_specs=[pl.BlockSpec((nb, tq, W), lambda n, i: (n, i, 0)), kvspec, kvspec],
        out_specs=pl.BlockSpec((nb, tq, W), lambda n, i: (n, i, 0)),
        out_shape=jax.ShapeDtypeStruct((N, Tq, W), BF16),
        compiler_params=_cparams("parallel", "arbitrary"),
        name="xattn",
    )(q, mk, mv)


def _samp_a_kernel(q_ref, kn_ref, vn_ref, kc_ref, vc_ref, o_ref, *, L, T):
    ng = len(A_GROUPS)
    rows = ng * T
    row_c = lax.broadcasted_iota(jnp.int32, (rows, L), 0)
    col_c = lax.broadcasted_iota(jnp.int32, (rows, L), 1)
    row_n = lax.broadcasted_iota(jnp.int32, (rows, T), 0)
    col_n = lax.broadcasted_iota(jnp.int32, (rows, T), 1)

    def per_group(row, vals):
        out = jnp.full(row.shape, vals[-1], jnp.int32)
        for gi in range(ng - 2, -1, -1):
            out = jnp.where(row < (gi + 1) * T, vals[gi], out)
        return out

    def masks(row, col, is_cache):
        tt = row - per_group(row, [gi * T for gi in range(ng)])
        delta = (L + tt - col) if is_cache else (tt - col)
        dil_m1 = per_group(row, [d - 1 for _, d in A_GROUPS])
        win = per_group(row, [w for w, _ in A_GROUPS])
        return (delta >= 0) & ((delta & dil_m1) == 0) & (delta <= win)

    mask_c = masks(row_c, col_c, True)
    mask_n = masks(row_n, col_n, False)
    for h in range(A_KV_HEADS):
        hs = slice(h * HEAD_DIM_A, (h + 1) * HEAD_DIM_A)
        q = jnp.concatenate([q_ref[0, :, g * W_A + h * HEAD_DIM_A: g * W_A + (h + 1) * HEAD_DIM_A]
                             for g in range(ng)], axis=0).astype(BF16)
        s_c = jnp.where(mask_c, _dot_nt(q, kc_ref[0, :, hs].astype(BF16)), NEG)
        s_n = jnp.where(mask_n, _dot_nt(q, kn_ref[0, :, hs].astype(BF16)), NEG)
        m = jnp.maximum(jnp.max(s_c, axis=-1, keepdims=True), jnp.max(s_n, axis=-1, keepdims=True))
        p_c = jnp.exp(s_c - m)
        p_n = jnp.exp(s_n - m)
        den = jnp.sum(p_c, axis=-1, keepdims=True) + jnp.sum(p_n, axis=-1, keepdims=True)
        acc = (_dot(p_c.astype(BF16), vc_ref[0, :, hs].astype(BF16))
               + _dot(p_n.astype(BF16), vn_ref[0, :, hs].astype(BF16)))
        o = acc / den
        lse = m + jnp.log(den)
        lses = [lse[g * T:(g + 1) * T] for g in range(ng)]
        top = functools.reduce(jnp.maximum, lses)
        ws = [jnp.exp(l - top) for l in lses]
        wsum = functools.reduce(jnp.add, ws)
        merged = functools.reduce(jnp.add, [(ws[g] / wsum) * o[g * T:(g + 1) * T] for g in range(ng)])
        o_ref[0, :, hs] = merged.astype(BF16)


def _samp_a(qa, ka, va, cache_k, cache_v):
    N, T, _ = qa.shape
    L = cache_k.shape[1]
    new = pl.BlockSpec((1, T, W_A), lambda n: (n, 0, 0))
    cache = pl.BlockSpec((1, L, W_A), lambda n: (n, 0, 0))
    return pl.pallas_call(
        functools.partial(_samp_a_kernel, L=L, T=T),
        grid=(N,),
        in_specs=[pl.BlockSpec((1, T, QA_W), lambda n: (n, 0, 0)), new, new, cache, cache],
        out_specs=new,
        out_shape=jax.ShapeDtypeStruct((N, T, W_A), BF16),
        compiler_params=_cparams("parallel"),
        name="samp_a",
    )(qa, ka, va, cache_k, cache_v)


def _samp_b_kernel(sink_ref, q_ref, kvn_ref, kc_ref, vc_ref, o_ref, *, L, T, nb):
    row_c = lax.broadcasted_iota(jnp.int32, (T, L), 0)
    col_c = lax.broadcasted_iota(jnp.int32, (T, L), 1)
    delta_c = L + row_c - col_c
    mask_c = (delta_c >= 0) & (delta_c <= B_WINDOW)
    row_n = lax.broadcasted_iota(jnp.int32, (T, T), 0)
    col_n = lax.broadcasted_iota(jnp.int32, (T, T), 1)
    mask_n = col_n <= row_n
    lo = lax.broadcasted_iota(jnp.int32, (T, LANES), 1) < HEAD_DIM_B
    pairs = B_Q_HEADS // B_KV_HEADS // 2

    def body(n, carry):
        for kv in range(B_KV_HEADS):
            kc = _dup_half(kc_ref[n], kv).astype(BF16)
            vc = _dup_half(vc_ref[n], kv).astype(BF16)
            kn = _dup_half(kvn_ref[n, :, 0:LANES], kv).astype(BF16)
            vn = _dup_half(kvn_ref[n, :, LANES:2 * LANES], kv).astype(BF16)
            for m in range(pairs):
                cs = slice((kv * pairs + m) * LANES, (kv * pairs + m + 1) * LANES)
                qp = q_ref[n, :, cs]
                outs = []
                for par in range(2):
                    qh = jnp.where(lo if par == 0 else jnp.logical_not(lo), qp, 0.0).astype(BF16)
                    sink = sink_ref[kv * 2 * pairs + 2 * m + par]
                    s_c = jnp.where(mask_c, _dot_nt(qh, kc), NEG)
                    s_n = jnp.where(mask_n, _dot_nt(qh, kn), NEG)
                    mx = jnp.maximum(jnp.max(s_c, axis=-1, keepdims=True),
                                     jnp.max(s_n, axis=-1, keepdims=True))
                    mx = jnp.maximum(mx, sink)
                    p_c = jnp.exp(s_c - mx)
                    p_n = jnp.exp(s_n - mx)
                    den = (jnp.sum(p_c, axis=-1, keepdims=True) + jnp.sum(p_n, axis=-1, keepdims=True)
                           + jnp.exp(sink - mx))
                    acc = _dot(p_c.astype(BF16), vc) + _dot(p_n.astype(BF16), vn)
                    outs.append(acc / den)
                o_ref[n, :, cs] = jnp.where(lo, outs[0], outs[1]).astype(BF16)
        return carry

    lax.fori_loop(0, nb, body, 0)


def _samp_b(sink, qb, kvb, cache_k, cache_v, nb=8):
    N, T, _ = qb.shape
    L = cache_k.shape[1]
    cache = pl.BlockSpec((nb, L, LANES), lambda n: (n, 0, 0))
    return pl.pallas_call(
        functools.partial(_samp_b_kernel, L=L, T=T, nb=nb),
        grid=(N // nb,),
        in_specs=[pl.BlockSpec(memory_space=pltpu.SMEM),
                  pl.BlockSpec((nb, T, QB_W), lambda n: (n, 0, 0)),
                  pl.BlockSpec((nb, T, KVB_W), lambda n: (n, 0, 0)), cache, cache],
        out_specs=pl.BlockSpec((nb, T, QB_W), lambda n: (n, 0, 0)),
        out_shape=jax.ShapeDtypeStruct((N, T, QB_W), BF16),
        compiler_params=_cparams("parallel"),
        name="samp_b",
    )(sink, qb, kvb, cache_k, cache_v)


def _finish_kernel(*refs, combine):
    if combine:
        (x_ref, o0, o1, o2, l0, l1, l2, ob_ref, oc_ref, gt_ref,
         wa_ref, wb_ref, wc_ref, wo_ref, gf_ref, x1_ref, h2_ref) = refs
        lses = [l0[...], l1[...], l2[...]]
        top = functools.reduce(jnp.maximum, lses)
        ws = [jnp.exp(l - top) for l in lses]
        wsum = functools.reduce(jnp.add, ws)
        oa = (ws[0] / wsum) * o0[...] + (ws[1] / wsum) * o1[...] + (ws[2] / wsum) * o2[...]
        oa = oa.astype(BF16)
    else:
        (x_ref, oa_ref, ob_ref, oc_ref, gt_ref,
         wa_ref, wb_ref, wc_ref, wo_ref, gf_ref, x1_ref, h2_ref) = refs
        oa = oa_ref[...]
    D = x_ref.shape[1]
    merged = (gt_ref[:, 0:D].astype(F32) * _dot(oa, wa_ref[...])
              + gt_ref[:, D:2 * D].astype(F32) * _dot(ob_ref[...], wb_ref[...])
              + gt_ref[:, 2 * D:3 * D].astype(F32) * _dot(oc_ref[...], wc_ref[...]))
    x1 = x_ref[...] + _dot(merged.astype(BF16), wo_ref[...])
    x1_ref[...] = x1
    ms = jnp.mean(x1 * x1, axis=-1, keepdims=True)
    h2_ref[...] = (x1 * lax.rsqrt(ms + RMS_EPS) * gf_ref[...]).astype(BF16)


def _finish(x2d, oa_parts, ob, oc, gt, wa, wb, wc, wo, g_ffn, tm=256):
    R, D = x2d.shape
    tm = min(tm, R)
    combine = len(oa_parts) > 1
    row = lambda w: pl.BlockSpec((tm, w), lambda i: (i, 0))
    full = lambda a: pl.BlockSpec(a.shape, lambda i: (0, 0))
    gf = g_ffn.reshape(1, D)
    ins = [x2d, *oa_parts, ob, oc, gt, wa, wb, wc, wo, gf]
    specs = ([row(D)] + [row(W_A)] * len(oa_parts) + [row(QB_W), row(QC_W), row(gt.shape[1])]
             + [full(wa), full(wb), full(wc), full(wo), full(gf)])
    return pl.pallas_call(
        functools.partial(_finish_kernel, combine=combine),
        grid=(R // tm,),
        in_specs=specs,
        out_specs=[row(D), row(D)],
        out_shape=[jax.ShapeDtypeStruct((R, D), F32), jax.ShapeDtypeStruct((R, D), BF16)],
        compiler_params=_cparams("parallel"),
        name="finish",
    )(*ins)


def _peer_scores_kernel(h_ref, wq_ref, k1_ref, k2_ref, s1_ref, s2_ref):
    q = _dot(h_ref[...], wq_ref[...]).astype(BF16)
    half = k1_ref.shape[1]
    for h in range(PEER_HEADS):
        c0 = h * 2 * half
        s1_ref[h] = _dot_nt(k1_ref[...], q[:, c0:c0 + half])
        s2_ref[h] = _dot_nt(k2_ref[...], q[:, c0 + half:c0 + 2 * half])


def _peer_scores(h2, wq, k1, k2, tm=512):
    R, D = h2.shape
    tm = min(tm, R)
    full = lambda a: pl.BlockSpec(a.shape, lambda i: (0, 0))
    out = pl.BlockSpec((PEER_HEADS, PEER_N_KEYS, tm), lambda i: (0, 0, i))
    return pl.pallas_call(
        _peer_scores_kernel,
        grid=(R // tm,),
        in_specs=[pl.BlockSpec((tm, D), lambda i: (i, 0)), full(wq), full(k1), full(k2)],
        out_specs=[out, out],
        out_shape=[jax.ShapeDtypeStruct((PEER_HEADS, PEER_N_KEYS, R), F32)] * 2,
        compiler_params=_cparams("parallel"),
        name="peer_scores",
    )(h2, wq, k1, k2)


_CAND_PAIRS = [(a, b) for a in range(PEER_TOPK) for b in range(PEER_TOPK) if (a + 1) * (b + 1) <= PEER_TOPK]


def _peer_topk_kernel(s1_ref, s2_ref, tau_ref, c1_ref, e2_ref, v_ref):
    for half, s_ref in enumerate((s1_ref, s2_ref)):
        for h in range(PEER_HEADS):
            s = s_ref[h]
            for k in range(PEER_TOPK):
                m = jnp.max(s, axis=0, keepdims=True)
                v_ref[half, k, h:h + 1, :] = m
                s = jnp.where(s == m, NEG, s)
    v1 = [v_ref[0, k] for k in range(PEER_TOPK)]
    v2 = [v_ref[1, k] for k in range(PEER_TOPK)]
    cands = [v1[a] + v2[b] for a, b in _CAND_PAIRS]
    work = list(cands)
    tau = None
    for k in range(PEER_TOPK):
        tau = functools.reduce(jnp.maximum, work)
        if k + 1 < PEER_TOPK:
            work = [jnp.where(w == tau, NEG, w) for w in work]
    top = v1[0] + v2[0]
    z = functools.reduce(jnp.add, [jnp.where(c >= tau, jnp.exp(c - top), 0.0) for c in cands])
    tau_ref[...] = tau
    inv_z = 1.0 / z
    for h in range(PEER_HEADS):
        c1_ref[h] = jnp.exp(s1_ref[h] - v1[0][h:h + 1, :]) * inv_z[h:h + 1, :]
        e2_ref[h] = jnp.exp(s2_ref[h] - v2[0][h:h + 1, :])


def _peer_topk(s1, s2, tt=256):
    _, _, R = s1.shape
    tt = min(tt, R)
    big = pl.BlockSpec((PEER_HEADS, PEER_N_KEYS, tt), lambda i: (0, 0, i))
    return pl.pallas_call(
        _peer_topk_kernel,
        grid=(R // tt,),
        in_specs=[big, big],
        out_specs=[pl.BlockSpec((PEER_HEADS, tt), lambda i: (0, i)), big, big],
        out_shape=[jax.ShapeDtypeStruct((PEER_HEADS, R), F32),
                   jax.ShapeDtypeStruct(s1.shape, F32), jax.ShapeDtypeStruct(s1.shape, F32)],
        scratch_shapes=[pltpu.VMEM((2, PEER_TOPK, PEER_HEADS, tt), F32)],
        compiler_params=_cparams("parallel"),
        name="peer_topk",
    )(s1, s2)


def _peer_dense_kernel(h_ref, u_ref, v_ref, s1_ref, c1_ref, s2_ref, e2_ref, tau_ref, o_ref, *, ni):
    e = pl.program_id(1)

    @pl.when(e == 0)
    def _():
        o_ref[...] = jnp.zeros_like(o_ref)

    a_t = _dot_nt(u_ref[...], h_ref[...])
    act = 0.5 * a_t * (1.0 + lax.erf(a_t * (2.0 ** -0.5)))
    parts = []
    for ii in range(ni):
        i = e * ni + ii
        w = None
        for h in range(PEER_HEADS):
            s1row = s1_ref[h, pl.ds(i, 1), :]
            c1row = c1_ref[h, pl.ds(i, 1), :]
            sel = (s1row + s2_ref[h]) >= tau_ref[h:h + 1, :]
            contrib = jnp.where(sel, c1row * e2_ref[h], 0.0)
            w = contrib if w is None else w + contrib
        parts.append(w * act[ii * PEER_N_KEYS:(ii + 1) * PEER_N_KEYS])
    p_t = jnp.concatenate(parts, axis=0).astype(BF16)
    o_ref[...] += _dot_tn(p_t, v_ref[...])


def _peer_dense(h2, u_bf, v_bf, s1, c1, s2, e2, tau, tt=512, ni=4):
    R, D = h2.shape
    tt = min(tt, R)
    ec = ni * PEER_N_KEYS
    n_e = u_bf.shape[0] // ec
    big = pl.BlockSpec((PEER_HEADS, PEER_N_KEYS, tt), lambda t, e: (0, 0, t))
    wspec = pl.BlockSpec((ec, D), lambda t, e: (e, 0))
    return pl.pallas_call(
        functools.partial(_peer_dense_kernel, ni=ni),
        grid=(R // tt, n_e),
        in_specs=[pl.BlockSpec((tt, D), lambda t, e: (t, 0)), wspec, wspec, big, big, big, big,
                  pl.BlockSpec((PEER_HEADS, tt), lambda t, e: (0, t))],
        out_specs=pl.BlockSpec((tt, D), lambda t, e: (t, 0)),
        out_shape=jax.ShapeDtypeStruct((R, D), F32),
        compiler_params=_cparams("parallel", "arbitrary"),
        name="peer_dense",
    )(h2, u_bf, v_bf, s1, c1, s2, e2, tau)


def _layer_tail(x2d, oa_parts, ob, oc, gt, wts):
    (wa, wb, wc, wo, g_ffn, wq, k1, k2, u_bf, v_bf, g_final) = wts
    x1, h2 = _finish(x2d, oa_parts, ob, oc, gt, wa, wb, wc, wo, g_ffn)
    s1, s2 = _peer_scores(h2, wq, k1, k2)
    tau, c1, e2 = _peer_topk(s1, s2)
    peer = _peer_dense(h2, u_bf, v_bf, s1, c1, s2, e2, tau)
    return _add_rmsnorm(x1, peer, g_final)


def kernel(x_prompt, x_sample, cache_a_k, cache_a_v, cache_b_k, cache_b_v, cache_mem_k, cache_mem_v,
           mem_prompt, g_attn, w_in, sink_b, w_br_a, w_br_b, w_br_c, w_o, g_mem, w_mem_kv, g_ffn,
           w_peer_q, peer_k1, peer_k2, peer_u, peer_v, g_final):
    B, S, D = x_prompt.shape
    N, T, _ = x_sample.shape
    depth = w_in.shape[0]
    assert depth == 1 and S % (A_GROUPS[-1][1] * BLK) == 0
    l = 0
    la_p = min(A_GROUPS[-1][0], S)
    lb_p = min(B_WINDOW, S)

    w_in_bf = w_in[l].astype(BF16)
    wts = (w_br_a[l].astype(BF16), w_br_b[l].astype(BF16), w_br_c[l].astype(BF16), w_o[l].astype(BF16),
           g_ffn[l], w_peer_q[l].astype(BF16), peer_k1[l].astype(BF16), peer_k2[l].astype(BF16),
           peer_u[l].astype(BF16), peer_v[l].astype(BF16), g_final)
    sink = sink_b[l].astype(F32)

    xp = x_prompt.reshape(B * S, D)
    hp = _rmsnorm(xp, g_attn[l], BF16)
    qa, ka, va, qb, kvb, qc, gt = _in_proj(hp, w_in_bf, _rope_tables(jnp.arange(S, dtype=jnp.int32)), tm=1024)
    parts = [_attn_a_group(qa, ka, va, B, S, g, d) for g, (_, d) in enumerate(A_GROUPS)]
    o_parts = [p[0] for p in parts] + [p[1] for p in parts]
    ob = _attn_b(sink, qb, kvb, B, S)
    M = mem_prompt.shape[1]
    hm = _rmsnorm(mem_prompt.reshape(B * M, D), g_mem[l], BF16, tm=256)
    mkv = _matmul_f32(hm, w_mem_kv[l].astype(BF16), tm=256, tn=512)
    mk = mkv[:, :QC_W].reshape(B, M, QC_W)
    mv = mkv[:, QC_W:].reshape(B, M, QC_W)
    oc = _xattn(qc.reshape(B, S, QC_W), mk, mv, nb=1, tq=1024).reshape(B * S, QC_W)
    y_prompt = _layer_tail(xp, o_parts, ob, oc, gt, wts).reshape(B, S, D)

    ka4 = ka.reshape(B, S, A_KV_HEADS, HEAD_DIM_A)
    va4 = va.reshape(B, S, A_KV_HEADS, HEAD_DIM_A)
    kvb4 = kvb.reshape(B, S, 2, B_KV_HEADS, HEAD_DIM_B)
    new_p = (ka4[:, S - la_p:][None], va4[:, S - la_p:][None],
             kvb4[:, S - lb_p:, 0][None], kvb4[:, S - lb_p:, 1][None],
             mk.reshape(B, M, C_HEADS, HEAD_DIM_C)[None], mv.reshape(B, M, C_HEADS, HEAD_DIM_C)[None])

    xs = x_sample.reshape(N * T, D)
    hs = _rmsnorm(xs, g_attn[l], BF16)
    pos_s = jnp.tile(PAST_LEN + jnp.arange(T, dtype=jnp.int32), N)
    qa, ka, va, qb, kvb, qc, gt = _in_proj(hs, w_in_bf, _rope_tables(pos_s), tm=N * T)
    oa = _samp_a(qa.astype(F32).reshape(N, T, QA_W), ka.reshape(N, T, W_A), va.reshape(N, T, W_A),
                 cache_a_k[l].reshape(N, -1, W_A), cache_a_v[l].reshape(N, -1, W_A)).reshape(N * T, W_A)
    ob = _samp_b(sink, qb.astype(F32).reshape(N, T, QB_W), kvb.reshape(N, T, KVB_W),
                 cache_b_k[l].reshape(N, -1, LANES), cache_b_v[l].reshape(N, -1, LANES)).reshape(N * T, QB_W)
    oc = _xattn(qc.astype(F32).reshape(N, T, QC_W), cache_mem_k[l].reshape(N, -1, QC_W),
                cache_mem_v[l].reshape(N, -1, QC_W), nb=8, tq=T).reshape(N * T, QC_W)
    y_sample = _layer_tail(xs, [oa], ob, oc, gt, wts).reshape(N, T, D)

    ka4 = ka.reshape(N, T, A_KV_HEADS, HEAD_DIM_A)
    va4 = va.reshape(N, T, A_KV_HEADS, HEAD_DIM_A)
    kvb4 = kvb.reshape(N, T, 2, B_KV_HEADS, HEAD_DIM_B)
    new_s = (ka4[None], va4[None], kvb4[:, :, 0][None], kvb4[:, :, 1][None])

    return (y_prompt, y_sample) + new_p + new_s
```

```python
import functools

import jax
import jax.numpy as jnp
from jax import lax
from jax.experimental import pallas as pl
from jax.experimental.pallas import tpu as pltpu

F32 = jnp.float32
BF16 = jnp.bfloat16

PAST_LEN = 16384
A_GROUPS = ((128, 1), (512, 4), (2048, 16))
A_DILS = tuple(d for _, d in A_GROUPS)
A_KV_HEADS = 4
HEAD_DIM_A = 128
B_WINDOW = 128
B_Q_HEADS = 16
B_KV_HEADS = 2
HEAD_DIM_B = 64
C_HEADS = 4
HEAD_DIM_C = 128
BLK = 128
ROPE_THETA = 10000.0
RMS_EPS = 1e-6
PEER_HEADS = 8
PEER_N_KEYS = 128
PEER_TOPK = 16

LANES = 128
NEG = -1e30
VMEM_LIMIT = 56 * 1024 * 1024

W_A = A_KV_HEADS * HEAD_DIM_A
QA_W = len(A_GROUPS) * W_A
QB_W = B_Q_HEADS * HEAD_DIM_B
KVB_W = 2 * B_KV_HEADS * HEAD_DIM_B
QC_W = C_HEADS * HEAD_DIM_C
IN_TN = 256
IN_SLABS = IN_TN // LANES


def _cparams(*sem):
    return pltpu.CompilerParams(dimension_semantics=sem, vmem_limit_bytes=VMEM_LIMIT)


def _dot_nt(a, b):
    return lax.dot_general(a, b, (((1,), (1,)), ((), ())), preferred_element_type=F32)


def _dot_tn(a, b):
    return lax.dot_general(a, b, (((0,), (0,)), ((), ())), preferred_element_type=F32)


def _dot(a, b):
    return jnp.dot(a, b, preferred_element_type=F32)


def _bdot_nt(a, b):
    return lax.dot_general(a, b, (((2,), (2,)), ((0,), (0,))), preferred_element_type=F32)


def _bdot(a, b):
    return lax.dot_general(a, b, (((2,), (1,)), ((0,), (0,))), preferred_element_type=F32)


def _strided_rows(ref, r, n, d):
    if d == 1:
        return ref[...]
    return ref[pl.ds(r, n, stride=d), :]


def _rms_kernel(x_ref, g_ref, o_ref):
    x = x_ref[...]
    ms = jnp.mean(x * x, axis=-1, keepdims=True)
    o_ref[...] = (x * lax.rsqrt(ms + RMS_EPS) * g_ref[...]).astype(o_ref.dtype)


def _rmsnorm(x2d, g, out_dtype, tm=512):
    R, D = x2d.shape
    tm = min(tm, R)
    return pl.pallas_call(
        _rms_kernel,
        grid=(R // tm,),
        in_specs=[pl.BlockSpec((tm, D), lambda i: (i, 0)), pl.BlockSpec((1, D), lambda i: (0, 0))],
        out_specs=pl.BlockSpec((tm, D), lambda i: (i, 0)),
        out_shape=jax.ShapeDtypeStruct((R, D), out_dtype),
        compiler_params=_cparams("parallel"),
        name="rmsnorm",
    )(x2d, g.reshape(1, D))


def _add_rms_kernel(a_ref, b_ref, g_ref, o_ref):
    x = a_ref[...] + b_ref[...]
    ms = jnp.mean(x * x, axis=-1, keepdims=True)
    o_ref[...] = x * lax.rsqrt(ms + RMS_EPS) * g_ref[...]


def _add_rmsnorm(a, b, g, tm=512):
    R, D = a.shape
    tm = min(tm, R)
    return pl.pallas_call(
        _add_rms_kernel,
        grid=(R // tm,),
        in_specs=[pl.BlockSpec((tm, D), lambda i: (i, 0)), pl.BlockSpec((tm, D), lambda i: (i, 0)),
                  pl.BlockSpec((1, D), lambda i: (0, 0))],
        out_specs=pl.BlockSpec((tm, D), lambda i: (i, 0)),
        out_shape=jax.ShapeDtypeStruct((R, D), F32),
        compiler_params=_cparams("parallel"),
        name="add_rmsnorm",
    )(a, b, g.reshape(1, D))


def _mm_kernel(a_ref, b_ref, o_ref):
    o_ref[...] = _dot(a_ref[...], b_ref[...])


def _matmul_f32(a, b, tm, tn):
    M, K = a.shape
    N = b.shape[1]
    return pl.pallas_call(
        _mm_kernel,
        grid=(M // tm, N // tn),
        in_specs=[pl.BlockSpec((tm, K), lambda i, j: (i, 0)), pl.BlockSpec((K, tn), lambda i, j: (0, j))],
        out_specs=pl.BlockSpec((tm, tn), lambda i, j: (i, j)),
        out_shape=jax.ShapeDtypeStruct((M, N), F32),
        compiler_params=_cparams("parallel", "arbitrary"),
        name="matmul",
    )(a, b)


_T_QA = (0, QA_W // IN_TN)
_T_KA = (_T_QA[1], _T_QA[1] + W_A // IN_TN)
_T_VA = (_T_KA[1], _T_KA[1] + W_A // IN_TN)
_T_QB = (_T_VA[1], _T_VA[1] + QB_W // IN_TN)
_T_KVB = (_T_QB[1], _T_QB[1] + KVB_W // IN_TN)
_T_QC = (_T_KVB[1], _T_KVB[1] + QC_W // IN_TN)
_TILES_PER_GROUP = W_A // IN_TN


def _in_kernel(*refs, dilated, tm):
    h_ref, w_ref, ca_ref, sa_ref, cb_ref, sb_ref = refs[:6]
    z_ref = refs[-1]
    outs = refs[6:-1]
    if dilated:
        qa_refs, (kaf_ref, vaf_ref), ka_refs, va_refs = outs[0:3], outs[3:5], outs[5:8], outs[8:11]
        qb_ref, kvb_ref, qc_ref, gt_ref = outs[11:]
    else:
        qa_ref, kaf_ref, vaf_ref, qb_ref, kvb_ref, qc_ref, gt_ref = outs
    j = pl.program_id(1)
    z = _dot(h_ref[...], w_ref[...])
    for c in range(IN_SLABS):
        z_ref[c] = z[:, c * LANES:(c + 1) * LANES]
    lanes = [slice(c * LANES, (c + 1) * LANES) for c in range(IN_SLABS)]

    def rope_a(c, d=1, r=0):
        n = tm // d
        zh = _strided_rows(z_ref.at[c], r, n, d)
        return (zh * _strided_rows(ca_ref, r, n, d)
                + pltpu.roll(zh, HEAD_DIM_A // 2, 1) * _strided_rows(sa_ref, r, n, d))

    def rope_b(c):
        zh = z_ref[c]
        lane = lax.broadcasted_iota(jnp.int32, zh.shape, 1)
        first = (lane & (HEAD_DIM_B // 2)) == 0
        partner = jnp.where(first, pltpu.roll(zh, LANES - HEAD_DIM_B // 2, 1),
                            pltpu.roll(zh, HEAD_DIM_B // 2, 1))
        return zh * cb_ref[...] + partner * sb_ref[...]

    def dil_col(r, part, c):
        return r * W_A + part * IN_TN + c * LANES

    for jj in range(*_T_QA):
        g, part = divmod(jj, _TILES_PER_GROUP)

        @pl.when(j == jj)
        def _(g=g, part=part):
            scale = HEAD_DIM_A ** -0.5
            for c in range(IN_SLABS):
                if dilated:
                    d = A_DILS[g]
                    for r in range(d):
                        col = dil_col(r, part, c)
                        qa_refs[g][:, col:col + LANES] = (rope_a(c, d, r) * scale).astype(BF16)
                else:
                    qa_ref[:, lanes[c]] = rope_a(c) * scale

    for jj in range(*_T_KA):
        @pl.when(j == jj)
        def _(part=jj - _T_KA[0]):
            for c in range(IN_SLABS):
                full = rope_a(c)
                kaf_ref[:, lanes[c]] = full
                if dilated:
                    ka_refs[0][:, lanes[c]] = full.astype(BF16)
                    for g in range(1, len(A_DILS)):
                        d = A_DILS[g]
                        for r in range(d):
                            col = dil_col(r, part, c)
                            ka_refs[g][:, col:col + LANES] = rope_a(c, d, r).astype(BF16)

    for jj in range(*_T_VA):
        @pl.when(j == jj)
        def _(part=jj - _T_VA[0]):
            for c in range(IN_SLABS):
                full = z_ref[c]
                vaf_ref[:, lanes[c]] = full
                if dilated:
                    va_refs[0][:, lanes[c]] = full.astype(BF16)
                    for g in range(1, len(A_DILS)):
                        d = A_DILS[g]
                        for r in range(d):
                            col = dil_col(r, part, c)
                            va_refs[g][:, col:col + LANES] = _strided_rows(
                                z_ref.at[c], r, tm // d, d).astype(BF16)

    @pl.when((j >= _T_QB[0]) & (j < _T_QB[1]))
    def _():
        for c in range(IN_SLABS):
            qb_ref[:, lanes[c]] = (rope_b(c) * (HEAD_DIM_B ** -0.5)).astype(qb_ref.dtype)

    @pl.when(j == _T_KVB[0])
    def _():
        kvb_ref[:, lanes[0]] = rope_b(0)
        kvb_ref[:, lanes[1]] = z_ref[1]

    @pl.when((j >= _T_QC[0]) & (j < _T_QC[1]))
    def _():
        for c in range(IN_SLABS):
            qc_ref[:, lanes[c]] = (z_ref[c] * (HEAD_DIM_C ** -0.5)).astype(qc_ref.dtype)

    @pl.when(j >= _T_QC[1])
    def _():
        for c in range(IN_SLABS):
            gt_ref[:, lanes[c]] = jax.nn.sigmoid(z_ref[c]).astype(BF16)


def _in_proj(h, w_bf, tabs, tm, dilated):
    R, D = h.shape
    n_w = w_bf.shape[1]
    n_tiles = n_w // IN_TN
    n_tab = tabs[0].shape[0] // tm
    gates_w = n_w - _T_QC[1] * IN_TN
    qdt = BF16 if dilated else F32

    def seg(lo, hi, dtype, width):
        spec = pl.BlockSpec((tm, IN_TN), lambda i, j: (i, jnp.clip(j - lo, 0, hi - lo - 1)))
        return spec, jax.ShapeDtypeStruct((R, width), dtype)

    def dil(d):
        spec = pl.BlockSpec((tm // d, d * W_A), lambda i, j: (i, 0))
        return spec, jax.ShapeDtypeStruct((R // d, d * W_A), BF16)

    tab = pl.BlockSpec((tm, LANES), lambda i, j: (i % n_tab, 0))
    kaf, vaf = seg(*_T_KA, F32, W_A), seg(*_T_VA, F32, W_A)
    tail = [seg(*_T_QB, qdt, QB_W), seg(*_T_KVB, F32, KVB_W), seg(*_T_QC, qdt, QC_W),
            seg(_T_QC[1], n_tiles, BF16, gates_w)]
    if dilated:
        outs = ([dil(d) for d in A_DILS] + [kaf, vaf]
                + [seg(*_T_KA, BF16, W_A)] + [dil(d) for d in A_DILS[1:]]
                + [seg(*_T_VA, BF16, W_A)] + [dil(d) for d in A_DILS[1:]] + tail)
    else:
        outs = [seg(*_T_QA, F32, QA_W), kaf, vaf] + tail
    return pl.pallas_call(
        functools.partial(_in_kernel, dilated=dilated, tm=tm),
        grid=(R // tm, n_tiles),
        in_specs=[pl.BlockSpec((tm, D), lambda i, j: (i, 0)),
                  pl.BlockSpec((D, IN_TN), lambda i, j: (0, j)), tab, tab, tab, tab],
        out_specs=[o[0] for o in outs],
        out_shape=[o[1] for o in outs],
        scratch_shapes=[pltpu.VMEM((IN_SLABS, tm, LANES), F32)],
        compiler_params=_cparams("parallel", "arbitrary"),
        name="in_proj",
    )(h, w_bf, *tabs)


def _rope_tables(pos):
    def tab(hd):
        half = hd // 2
        inv = ROPE_THETA ** (-jnp.arange(half, dtype=F32) / half)
        ang = pos.astype(F32)[:, None] * inv[None, :]
        cos, sin = jnp.cos(ang), jnp.sin(ang)
        rep = LANES // hd
        return (jnp.tile(jnp.concatenate([cos, cos], -1), (1, rep)),
                jnp.tile(jnp.concatenate([-sin, sin], -1), (1, rep)))
    ca, sa = tab(HEAD_DIM_A)
    cb, sb = tab(HEAD_DIM_B)
    return ca, sa, cb, sb


def _band_masks():
    r = lax.broadcasted_iota(jnp.int32, (BLK, BLK), 0)
    c = lax.broadcasted_iota(jnp.int32, (BLK, BLK), 1)
    return c <= r, c >= r


def _band_softmax(q, kown, vown, kprev, vprev, own_mask, prev_mask, sink=None):
    s_o = jnp.where(own_mask, _dot_nt(q, kown), NEG)
    s_p = jnp.where(prev_mask, _dot_nt(q, kprev), NEG)
    m = jnp.maximum(jnp.max(s_o, axis=-1, keepdims=True), jnp.max(s_p, axis=-1, keepdims=True))
    if sink is not None:
        m = jnp.maximum(m, sink)
    p_o = jnp.exp(s_o - m)
    p_p = jnp.exp(s_p - m)
    den = jnp.sum(p_o, axis=-1, keepdims=True) + jnp.sum(p_p, axis=-1, keepdims=True)
    if sink is not None:
        den = den + jnp.exp(sink - m)
    acc = _dot(p_o.astype(BF16), vown) + _dot(p_p.astype(BF16), vprev)
    return acc / den, m, den


def _attn_a_kernel(q_ref, kc_ref, kp_ref, vc_ref, vp_ref, o_ref, l_ref, *, nsub):
    first = pl.program_id(2) == 0
    own_mask, prev_mask = _band_masks()
    prev_mask0 = prev_mask & jnp.logical_not(first)
    for h in range(A_KV_HEADS):
        hs = slice(h * HEAD_DIM_A, (h + 1) * HEAD_DIM_A)
        kprev = kp_ref[0, :, hs]
        vprev = vp_ref[0, :, hs]
        for b in range(nsub):
            rs = slice(b * BLK, (b + 1) * BLK)
            kown = kc_ref[0, rs, hs]
            vown = vc_ref[0, rs, hs]
            o, m, den = _band_softmax(q_ref[0, rs, hs], kown, vown, kprev, vprev,
                                      own_mask, prev_mask0 if b == 0 else prev_mask)
            o_ref[0, rs, hs] = o
            l_ref[0, rs, hs] = jnp.broadcast_to(m + jnp.log(den), (BLK, HEAD_DIM_A))
            kprev, vprev = kown, vown


def _attn_a_group(q, k, v, B, S, d, tq=512):
    Sd = S // d
    tq = min(tq, Sd)
    nsub = tq // BLK
    q3, k3, v3 = (a.reshape(B, Sd, d * W_A) for a in (q, k, v))
    cur = pl.BlockSpec((1, tq, W_A), lambda b, r, i: (b, i, r))
    prev = pl.BlockSpec((1, BLK, W_A), lambda b, r, i: (b, jnp.maximum(i * nsub - 1, 0), r))
    o, lse = pl.pallas_call(
        functools.partial(_attn_a_kernel, nsub=nsub),
        grid=(B, d, Sd // tq),
        in_specs=[cur, cur, prev, cur, prev],
        out_specs=[cur, cur],
        out_shape=[jax.ShapeDtypeStruct((B, Sd, d * W_A), F32)] * 2,
        compiler_params=_cparams("parallel", "parallel", "arbitrary"),
        name=f"attn_a_d{d}",
    )(q3, k3, k3, v3, v3)
    return o.reshape(B * Sd, d * W_A), lse.reshape(B * Sd, d * W_A)


def _dup_half(x, kv):
    lane = lax.broadcasted_iota(jnp.int32, x.shape, x.ndim - 1)
    lo = lane < HEAD_DIM_B
    rolled = pltpu.roll(x, HEAD_DIM_B, x.ndim - 1)
    return jnp.where(lo, x, rolled) if kv == 0 else jnp.where(lo, rolled, x)


def _attn_b_kernel(sink_ref, q_ref, kvc_ref, kvp_ref, o_ref, *, nsub):
    first = pl.program_id(1) == 0
    own_mask, prev_mask = _band_masks()
    prev_mask0 = prev_mask & jnp.logical_not(first)
    lo = lax.broadcasted_iota(jnp.int32, (BLK, LANES), 1) < HEAD_DIM_B
    pairs = B_Q_HEADS // B_KV_HEADS // 2
    for kv in range(B_KV_HEADS):
        kprev = _dup_half(kvp_ref[0, :, 0:LANES], kv).astype(BF16)
        vprev = _dup_half(kvp_ref[0, :, LANES:2 * LANES], kv).astype(BF16)
        for b in range(nsub):
            rs = slice(b * BLK, (b + 1) * BLK)
            kown = _dup_half(kvc_ref[0, rs, 0:LANES], kv).astype(BF16)
            vown = _dup_half(kvc_ref[0, rs, LANES:2 * LANES], kv).astype(BF16)
            for m in range(pairs):
                cs = slice((kv * pairs + m) * LANES, (kv * pairs + m + 1) * LANES)
                qp = q_ref[0, rs, cs]
                outs = []
                for par in range(2):
                    qh = jnp.where(lo if par == 0 else jnp.logical_not(lo), qp, jnp.zeros_like(qp))
                    sink = sink_ref[kv * 2 * pairs + 2 * m + par]
                    o, _, _ = _band_softmax(qh, kown, vown, kprev, vprev, own_mask,
                                            prev_mask0 if b == 0 else prev_mask, sink)
                    outs.append(o)
                o_ref[0, rs, cs] = jnp.where(lo, outs[0], outs[1]).astype(BF16)
            kprev, vprev = kown, vown


def _attn_b(sink, qb, kvb, B, S, tq=512):
    nsub = tq // BLK
    q3 = qb.reshape(B, S, QB_W)
    kv3 = kvb.reshape(B, S, KVB_W)
    o = pl.pallas_call(
        functools.partial(_attn_b_kernel, nsub=nsub),
        grid=(B, S // tq),
        in_specs=[pl.BlockSpec(memory_space=pltpu.SMEM),
                  pl.BlockSpec((1, tq, QB_W), lambda b, i: (b, i, 0)),
                  pl.BlockSpec((1, tq, KVB_W), lambda b, i: (b, i, 0)),
                  pl.BlockSpec((1, BLK, KVB_W), lambda b, i: (b, jnp.maximum(i * nsub - 1, 0), 0))],
        out_specs=pl.BlockSpec((1, tq, QB_W), lambda b, i: (b, i, 0)),
        out_shape=jax.ShapeDtypeStruct((B, S, QB_W), BF16),
        compiler_params=_cparams("parallel", "arbitrary"),
        name="attn_b",
    )(sink, q3, kv3, kv3)
    return o.reshape(B * S, QB_W)


def _xattn_kernel(q_ref, k_ref, v_ref, o_ref, *, nb, interleaved):
    M = k_ref.shape[1] // C_HEADS if interleaved else k_ref.shape[1]
    for n in range(nb):
        for h in range(C_HEADS):
            hs = slice(h * HEAD_DIM_C, (h + 1) * HEAD_DIM_C)
            if interleaved:
                k = k_ref[n, pl.ds(h, M, stride=C_HEADS), :]
                v = v_ref[n, pl.ds(h, M, stride=C_HEADS), :]
            else:
                k, v = k_ref[n, :, hs], v_ref[n, :, hs]
            s = _dot_nt(q_ref[n, :, hs].astype(BF16), k.astype(BF16))
            m = jnp.max(s, axis=-1, keepdims=True)
            p = jnp.exp(s - m)
            den = jnp.sum(p, axis=-1, keepdims=True)
            o = _dot(p.astype(BF16), v.astype(BF16)) / den
            o_ref[n, :, hs] = o.astype(o_ref.dtype)


def _xattn(q, mk, mv, nb, tq, interleaved):
    N, Tq, W = q.shape
    kvspec = pl.BlockSpec((nb,) + mk.shape[1:], lambda n, i: (n, 0, 0))
    return pl.pallas_call(
        functools.partial(_xattn_kernel, nb=nb, interleaved=interleaved),
        grid=(N // nb, Tq // tq),
        in_specs=[pl.BlockSpec((nb, tq, W), lambda n, i: (n, i, 0)), kvspec, kvspec],
        out_specs=pl.BlockSpec((nb, tq, W), lambda n, i: (n, i, 0)),
        out_shape=jax.ShapeDtypeStruct((N, Tq, W), BF16),
        compiler_params=_cparams("parallel", "arbitrary"),
        name="xattn",
    )(q, mk, mv)


def _samp_a_kernel(q_ref, kn_ref, vn_ref, kc_ref, vc_ref, o_ref, *, L, T):
    ng = len(A_GROUPS)
    rows = ng * T
    row_c = lax.broadcasted_iota(jnp.int32, (rows, L), 0)
    col_c = lax.broadcasted_iota(jnp.int32, (rows, L), 1)
    row_n = lax.broadcasted_iota(jnp.int32, (rows, T), 0)
    col_n = lax.broadcasted_iota(jnp.int32, (rows, T), 1)

    def per_group(row, vals):
        out = jnp.full(row.shape, vals[-1], jnp.int32)
        for gi in range(ng - 2, -1, -1):
            out = jnp.where(row < (gi + 1) * T, vals[gi], out)
        return out

    def masks(row, col, is_cache):
        tt = row - per_group(row, [gi * T for gi in range(ng)])
        delta = (L + tt - col) if is_cache else (tt - col)
        dil_m1 = per_group(row, [d - 1 for _, d in A_GROUPS])
        win = per_group(row, [w for w, _ in A_GROUPS])
        return (delta >= 0) & ((delta & dil_m1) == 0) & (delta <= win)

    mask_c = masks(row_c, col_c, True)
    mask_n = masks(row_n, col_n, False)
    for h in range(A_KV_HEADS):
        hs = slice(h * HEAD_DIM_A, (h + 1) * HEAD_DIM_A)
        q = jnp.concatenate([q_ref[0, :, g * W_A + h * HEAD_DIM_A: g * W_A + (h + 1) * HEAD_DIM_A]
                             for g in range(ng)], axis=0).astype(BF16)
        kc = kc_ref[0, pl.ds(h, L, stride=A_KV_HEADS), :].astype(BF16)
        vc = vc_ref[0, pl.ds(h, L, stride=A_KV_HEADS), :].astype(BF16)
        s_c = jnp.where(mask_c, _dot_nt(q, kc), NEG)
        s_n = jnp.where(mask_n, _dot_nt(q, kn_ref[0, :, hs].astype(BF16)), NEG)
        m = jnp.maximum(jnp.max(s_c, axis=-1, keepdims=True), jnp.max(s_n, axis=-1, keepdims=True))
        p_c = jnp.exp(s_c - m)
        p_n = jnp.exp(s_n - m)
        den = jnp.sum(p_c, axis=-1, keepdims=True) + jnp.sum(p_n, axis=-1, keepdims=True)
        acc = _dot(p_c.astype(BF16), vc) + _dot(p_n.astype(BF16), vn_ref[0, :, hs].astype(BF16))
        o = acc / den
        lse = m + jnp.log(den)
        lses = [lse[g * T:(g + 1) * T] for g in range(ng)]
        top = functools.reduce(jnp.maximum, lses)
        ws = [jnp.exp(l - top) for l in lses]
        wsum = functools.reduce(jnp.add, ws)
        merged = functools.reduce(jnp.add, [(ws[g] / wsum) * o[g * T:(g + 1) * T] for g in range(ng)])
        o_ref[0, :, hs] = merged.astype(BF16)


def _samp_a(qa, ka, va, cache_k, cache_v):
    N, T, _ = qa.shape
    L = cache_k.shape[1] // A_KV_HEADS
    new = pl.BlockSpec((1, T, W_A), lambda n: (n, 0, 0))
    cache = pl.BlockSpec((1, L * A_KV_HEADS, HEAD_DIM_A), lambda n: (n, 0, 0))
    return pl.pallas_call(
        functools.partial(_samp_a_kernel, L=L, T=T),
        grid=(N,),
        in_specs=[pl.BlockSpec((1, T, QA_W), lambda n: (n, 0, 0)), new, new, cache, cache],
        out_specs=new,
        out_shape=jax.ShapeDtypeStruct((N, T, W_A), BF16),
        compiler_params=_cparams("parallel"),
        name="samp_a",
    )(qa, ka, va, cache_k, cache_v)


def _samp_b_kernel(sink_ref, q_ref, kvn_ref, kc_ref, vc_ref, o_ref, *, L, T, nb):
    row_c = lax.broadcasted_iota(jnp.int32, (nb, T, L), 1)
    col_c = lax.broadcasted_iota(jnp.int32, (nb, T, L), 2)
    delta_c = L + row_c - col_c
    mask_c = (delta_c >= 0) & (delta_c <= B_WINDOW)
    row_n = lax.broadcasted_iota(jnp.int32, (nb, T, T), 1)
    col_n = lax.broadcasted_iota(jnp.int32, (nb, T, T), 2)
    mask_n = col_n <= row_n
    lo = lax.broadcasted_iota(jnp.int32, (nb, T, LANES), 2) < HEAD_DIM_B
    pairs = B_Q_HEADS // B_KV_HEADS // 2
    for kv in range(B_KV_HEADS):
        kc = _dup_half(kc_ref[...], kv).astype(BF16)
        vc = _dup_half(vc_ref[...], kv).astype(BF16)
        kn = _dup_half(kvn_ref[:, :, 0:LANES], kv).astype(BF16)
        vn = _dup_half(kvn_ref[:, :, LANES:2 * LANES], kv).astype(BF16)
        for m in range(pairs):
            cs = slice((kv * pairs + m) * LANES, (kv * pairs + m + 1) * LANES)
            qp = q_ref[:, :, cs]
            outs = []
            for par in range(2):
                qh = jnp.where(lo if par == 0 else jnp.logical_not(lo), qp, 0.0).astype(BF16)
                sink = sink_ref[kv * 2 * pairs + 2 * m + par]
                s_c = jnp.where(mask_c, _bdot_nt(qh, kc), NEG)
                s_n = jnp.where(mask_n, _bdot_nt(qh, kn), NEG)
                mx = jnp.maximum(jnp.max(s_c, axis=-1, keepdims=True),
                                 jnp.max(s_n, axis=-1, keepdims=True))
                mx = jnp.maximum(mx, sink)
                p_c = jnp.exp(s_c - mx)
                p_n = jnp.exp(s_n - mx)
                den = (jnp.sum(p_c, axis=-1, keepdims=True) + jnp.sum(p_n, axis=-1, keepdims=True)
                       + jnp.exp(sink - mx))
                acc = _bdot(p_c.astype(BF16), vc) + _bdot(p_n.astype(BF16), vn)
                outs.append(acc / den)
            o_ref[:, :, cs] = jnp.where(lo, outs[0], outs[1]).astype(BF16)


def _samp_b(sink, qb, kvb, cache_k, cache_v, nb=8):
    N, T, _ = qb.shape
    L = cache_k.shape[1]
    cache = pl.BlockSpec((nb, L, LANES), lambda n: (n, 0, 0))
    return pl.pallas_call(
        functools.partial(_samp_b_kernel, L=L, T=T, nb=nb),
        grid=(N // nb,),
        in_specs=[pl.BlockSpec(memory_space=pltpu.SMEM),
                  pl.BlockSpec((nb, T, QB_W), lambda n: (n, 0, 0)),
                  pl.BlockSpec((nb, T, KVB_W), lambda n: (n, 0, 0)), cache, cache],
        out_specs=pl.BlockSpec((nb, T, QB_W), lambda n: (n, 0, 0)),
        out_shape=jax.ShapeDtypeStruct((N, T, QB_W), BF16),
        compiler_params=_cparams("parallel"),
        name="samp_b",
    )(sink, qb, kvb, cache_k, cache_v)


def _finish_kernel(*refs, combine, tm):
    if combine:
        (x_ref, o0, o1, o2, l0, l1, l2, ob_ref, oc_ref, gt_ref,
         wa_ref, wb_ref, wc_ref, wo_ref, gf_ref, x1_ref, h2_ref, os_ref, ls_ref) = refs
        for gi, (o_ref, l_ref) in enumerate(((o1, l1), (o2, l2))):
            d = A_DILS[gi + 1]
            for r in range(d):
                for h in range(A_KV_HEADS):
                    cs = slice(r * W_A + h * HEAD_DIM_A, r * W_A + (h + 1) * HEAD_DIM_A)
                    os_ref.at[gi, h][pl.ds(r, tm // d, stride=d), :] = o_ref[:, cs]
                    ls_ref.at[gi, h][pl.ds(r, tm // d, stride=d), :] = l_ref[:, cs]
        heads = []
        for h in range(A_KV_HEADS):
            hs = slice(h * HEAD_DIM_A, (h + 1) * HEAD_DIM_A)
            lses = [l0[:, hs], ls_ref[0, h], ls_ref[1, h]]
            os = [o0[:, hs], os_ref[0, h], os_ref[1, h]]
            top = functools.reduce(jnp.maximum, lses)
            ws = [jnp.exp(l - top) for l in lses]
            wsum = functools.reduce(jnp.add, ws)
            heads.append(functools.reduce(jnp.add, [(w / wsum) * o for w, o in zip(ws, os)]).astype(BF16))
        oa = jnp.concatenate(heads, axis=1)
    else:
        (x_ref, oa_ref, ob_ref, oc_ref, gt_ref,
         wa_ref, wb_ref, wc_ref, wo_ref, gf_ref, x1_ref, h2_ref) = refs
        oa = oa_ref[...]
    D = x_ref.shape[1]
    merged = (gt_ref[:, 0:D].astype(F32) * _dot(oa, wa_ref[...])
              + gt_ref[:, D:2 * D].astype(F32) * _dot(ob_ref[...], wb_ref[...])
              + gt_ref[:, 2 * D:3 * D].astype(F32) * _dot(oc_ref[...], wc_ref[...]))
    x1 = x_ref[...] + _dot(merged.astype(BF16), wo_ref[...])
    x1_ref[...] = x1
    ms = jnp.mean(x1 * x1, axis=-1, keepdims=True)
    h2_ref[...] = (x1 * lax.rsqrt(ms + RMS_EPS) * gf_ref[...]).astype(BF16)


def _finish(x2d, oa_parts, ob, oc, gt, wa, wb, wc, wo, g_ffn, tm=256):
    R, D = x2d.shape
    tm = min(tm, R)
    combine = len(oa_parts) > 1
    row = lambda w: pl.BlockSpec((tm, w), lambda i: (i, 0))
    full = lambda a: pl.BlockSpec(a.shape, lambda i: (0, 0))
    gf = g_ffn.reshape(1, D)
    ins = [x2d, *oa_parts, ob, oc, gt, wa, wb, wc, wo, gf]
    if combine:
        dil = [pl.BlockSpec((tm // d, d * W_A), lambda i: (i, 0)) for d in A_DILS]
        oa_specs = dil + dil
        scratch = [pltpu.VMEM((len(A_DILS) - 1, A_KV_HEADS, tm, HEAD_DIM_A), F32)] * 2
    else:
        oa_specs = [row(W_A)]
        scratch = []
    specs = ([row(D)] + oa_specs + [row(QB_W), row(QC_W), row(gt.shape[1])]
             + [full(wa), full(wb), full(wc), full(wo), full(gf)])
    return pl.pallas_call(
        functools.partial(_finish_kernel, combine=combine, tm=tm),
        grid=(R // tm,),
        in_specs=specs,
        out_specs=[row(D), row(D)],
        out_shape=[jax.ShapeDtypeStruct((R, D), F32), jax.ShapeDtypeStruct((R, D), BF16)],
        scratch_shapes=scratch,
        compiler_params=_cparams("parallel"),
        name="finish",
    )(*ins)


def _peer_scores_kernel(h_ref, wq_ref, k1_ref, k2_ref, s1_ref, s2_ref):
    q = _dot(h_ref[...], wq_ref[...]).astype(BF16)
    half = k1_ref.shape[1]
    for h in range(PEER_HEADS):
        c0 = h * 2 * half
        s1_ref[h] = _dot_nt(k1_ref[...], q[:, c0:c0 + half])
        s2_ref[h] = _dot_nt(k2_ref[...], q[:, c0 + half:c0 + 2 * half])


def _peer_scores(h2, wq, k1, k2, tm=512):
    R, D = h2.shape
    tm = min(tm, R)
    full = lambda a: pl.BlockSpec(a.shape, lambda i: (0, 0))
    out = pl.BlockSpec((PEER_HEADS, PEER_N_KEYS, tm), lambda i: (0, 0, i))
    return pl.pallas_call(
        _peer_scores_kernel,
        grid=(R // tm,),
        in_specs=[pl.BlockSpec((tm, D), lambda i: (i, 0)), full(wq), full(k1), full(k2)],
        out_specs=[out, out],
        out_shape=[jax.ShapeDtypeStruct((PEER_HEADS, PEER_N_KEYS, R), F32)] * 2,
        compiler_params=_cparams("parallel"),
        name="peer_scores",
    )(h2, wq, k1, k2)


_CAND_PAIRS = [(a, b) for a in range(PEER_TOPK) for b in range(PEER_TOPK) if (a + 1) * (b + 1) <= PEER_TOPK]


def _peer_topk_kernel(s1_ref, s2_ref, thr_ref, c1_ref, e2_ref, v_ref):
    for half, s_ref in enumerate((s1_ref, s2_ref)):
        for h in range(PEER_HEADS):
            s = s_ref[h]
            for k in range(PEER_TOPK):
                m = jnp.max(s, axis=0, keepdims=True)
                v_ref[half, k, h:h + 1, :] = m
                s = jnp.where(s == m, NEG, s)
    v1 = [v_ref[0, k] for k in range(PEER_TOPK)]
    v2 = [v_ref[1, k] for k in range(PEER_TOPK)]
    cands = [v1[a] + v2[b] for a, b in _CAND_PAIRS]
    work = list(cands)
    tau = None
    for k in range(PEER_TOPK):
        tau = functools.reduce(jnp.maximum, work)
        if k + 1 < PEER_TOPK:
            work = [jnp.where(w == tau, NEG, w) for w in work]
    top = v1[0] + v2[0]
    z = functools.reduce(jnp.add, [jnp.where(c >= tau, jnp.exp(c - top), 0.0) for c in cands])
    inv_z = 1.0 / z
    for h in range(PEER_HEADS):
        hr = slice(h, h + 1)
        s1 = s1_ref[h]
        thr = jnp.full(s1.shape, -NEG, F32)
        for b in range(PEER_TOPK):
            thr = jnp.where(s1 + v2[b][hr, :] >= tau[hr, :], v2[b][hr, :], thr)
        thr_ref[h] = thr
        c1_ref[h] = jnp.exp(s1 - v1[0][hr, :]) * inv_z[hr, :]
        e2_ref[h] = jnp.exp(s2_ref[h] - v2[0][hr, :])


def _peer_topk(s1, s2, tt=256):
    _, _, R = s1.shape
    tt = min(tt, R)
    big = pl.BlockSpec((PEER_HEADS, PEER_N_KEYS, tt), lambda i: (0, 0, i))
    return pl.pallas_call(
        _peer_topk_kernel,
        grid=(R // tt,),
        in_specs=[big, big],
        out_specs=[big, big, big],
        out_shape=[jax.ShapeDtypeStruct(s1.shape, F32)] * 3,
        scratch_shapes=[pltpu.VMEM((2, PEER_TOPK, PEER_HEADS, tt), F32)],
        compiler_params=_cparams("parallel"),
        name="peer_topk",
    )(s1, s2)


def _peer_dense_kernel(h_ref, u_ref, v_ref, thr_ref, c1_ref, s2_ref, e2_ref, o_ref,
                       a0_ref, a1_ref, p0_ref, p1_ref, *, ni, n_k):
    k = pl.program_id(1)
    E = ni * PEER_N_KEYS

    @pl.when(k == 0)
    def _():
        a1_ref[...] = jnp.zeros_like(a1_ref)
        p0_ref[...] = jnp.zeros_like(p0_ref)
        o_ref[...] = jnp.zeros_like(o_ref)

    def gate(a_ref, p_ref, chunk):
        chunk = jnp.clip(chunk, 0, 2 * n_k - 1)
        for ii in range(ni):
            i = chunk * ni + ii
            rows = slice(ii * PEER_N_KEYS, (ii + 1) * PEER_N_KEYS)
            w = None
            for h in range(PEER_HEADS):
                sel = s2_ref[h] >= thr_ref[h, pl.ds(i, 1), :]
                contrib = jnp.where(sel, e2_ref[h], 0.0) * c1_ref[h, pl.ds(i, 1), :]
                w = contrib if w is None else w + contrib
            a = a_ref[rows, :]
            act = 0.5 * a * (1.0 + lax.erf(a * (2.0 ** -0.5)))
            p_ref[rows, :] = (w * act).astype(BF16)

    h = h_ref[...]
    o_ref[...] += _dot_tn(p0_ref[...], v_ref[0:E, :])
    a0_ref[...] = _dot_nt(u_ref[0:E, :], h)
    gate(a1_ref, p1_ref, 2 * k - 1)
    a1_ref[...] = _dot_nt(u_ref[E:2 * E, :], h)
    o_ref[...] += _dot_tn(p1_ref[...], v_ref[E:2 * E, :])
    gate(a0_ref, p0_ref, 2 * k)


def _peer_dense(h2, u_bf, v_bf, thr, c1, s2, e2, tt=512, ni=4):
    R, D = h2.shape
    tt = min(tt, R)
    E = ni * PEER_N_KEYS
    n_k = u_bf.shape[0] // (2 * E)
    big = pl.BlockSpec((PEER_HEADS, PEER_N_KEYS, tt), lambda t, k: (0, 0, t))
    return pl.pallas_call(
        functools.partial(_peer_dense_kernel, ni=ni, n_k=n_k),
        grid=(R // tt, n_k + 1),
        in_specs=[pl.BlockSpec((tt, D), lambda t, k: (t, 0)),
                  pl.BlockSpec((2 * E, D), lambda t, k: (jnp.minimum(k, n_k - 1), 0)),
                  pl.BlockSpec((2 * E, D), lambda t, k: (jnp.maximum(k - 1, 0), 0)),
                  big, big, big, big],
        out_specs=pl.BlockSpec((tt, D), lambda t, k: (t, 0)),
        out_shape=jax.ShapeDtypeStruct((R, D), F32),
        scratch_shapes=[pltpu.VMEM((E, tt), F32), pltpu.VMEM((E, tt), F32),
                        pltpu.VMEM((E, tt), BF16), pltpu.VMEM((E, tt), BF16)],
        compiler_params=_cparams("parallel", "arbitrary"),
        name="peer_dense",
    )(h2, u_bf, v_bf, thr, c1, s2, e2)


def _layer_tail(x2d, oa_parts, ob, oc, gt, wts):
    (wa, wb, wc, wo, g_ffn, wq, k1, k2, u_bf, v_bf, g_final) = wts
    x1, h2 = _finish(x2d, oa_parts, ob, oc, gt, wa, wb, wc, wo, g_ffn)
    s1, s2 = _peer_scores(h2, wq, k1, k2)
    thr, c1, e2 = _peer_topk(s1, s2)
    peer = _peer_dense(h2, u_bf, v_bf, thr, c1, s2, e2)
    return _add_rmsnorm(x1, peer, g_final)


def kernel(x_prompt, x_sample, cache_a_k, cache_a_v, cache_b_k, cache_b_v, cache_mem_k, cache_mem_v,
           mem_prompt, g_attn, w_in, sink_b, w_br_a, w_br_b, w_br_c, w_o, g_mem, w_mem_kv, g_ffn,
           w_peer_q, peer_k1, peer_k2, peer_u, peer_v, g_final):
    B, S, D = x_prompt.shape
    N, T, _ = x_sample.shape
    depth = w_in.shape[0]
    assert depth == 1 and S % (A_DILS[-1] * BLK) == 0
    assert all(d & (d - 1) == 0 for d in A_DILS)
    l = 0
    la_p = min(A_GROUPS[-1][0], S)
    lb_p = min(B_WINDOW, S)

    w_in_bf = w_in[l].astype(BF16)
    wts = (w_br_a[l].astype(BF16), w_br_b[l].astype(BF16), w_br_c[l].astype(BF16), w_o[l].astype(BF16),
           g_ffn[l], w_peer_q[l].astype(BF16), peer_k1[l].astype(BF16), peer_k2[l].astype(BF16),
           peer_u[l].astype(BF16), peer_v[l].astype(BF16), g_final)
    sink = sink_b[l].astype(F32)

    xp = x_prompt.reshape(B * S, D)
    hp = _rmsnorm(xp, g_attn[l], BF16)
    proj = _in_proj(hp, w_in_bf, _rope_tables(jnp.arange(S, dtype=jnp.int32)), tm=1024, dilated=True)
    qa_g, (ka, va), ka_g, va_g = proj[0:3], proj[3:5], proj[5:8], proj[8:11]
    qb, kvb, qc, gt = proj[11:]
    parts = [_attn_a_group(qa_g[g], ka_g[g], va_g[g], B, S, d) for g, d in enumerate(A_DILS)]
    o_parts = [p[0] for p in parts] + [p[1] for p in parts]
    ob = _attn_b(sink, qb, kvb, B, S)
    M = mem_prompt.shape[1]
    hm = _rmsnorm(mem_prompt.reshape(B * M, D), g_mem[l], BF16, tm=256)
    mkv = _matmul_f32(hm, w_mem_kv[l].astype(BF16), tm=256, tn=512)
    mk = mkv[:, :QC_W].reshape(B, M, QC_W)
    mv = mkv[:, QC_W:].reshape(B, M, QC_W)
    oc = _xattn(qc.reshape(B, S, QC_W), mk, mv, nb=1, tq=1024, interleaved=False).reshape(B * S, QC_W)
    y_prompt = _layer_tail(xp, o_parts, ob, oc, gt, wts).reshape(B, S, D)

    ka4 = ka.reshape(B, S, W_A)[:, S - la_p:].reshape(B, la_p, A_KV_HEADS, HEAD_DIM_A)
    va4 = va.reshape(B, S, W_A)[:, S - la_p:].reshape(B, la_p, A_KV_HEADS, HEAD_DIM_A)
    kvb4 = kvb.reshape(B, S, KVB_W)[:, S - lb_p:].reshape(B, lb_p, 2, B_KV_HEADS, HEAD_DIM_B)
    new_p = (ka4[None], va4[None], kvb4[:, :, 0][None], kvb4[:, :, 1][None],
             mk.reshape(B, M, C_HEADS, HEAD_DIM_C)[None], mv.reshape(B, M, C_HEADS, HEAD_DIM_C)[None])

    xs = x_sample.reshape(N * T, D)
    hs = _rmsnorm(xs, g_attn[l], BF16)
    pos_s = jnp.tile(PAST_LEN + jnp.arange(T, dtype=jnp.int32), N)
    qa, ka, va, qb, kvb, qc, gt = _in_proj(hs, w_in_bf, _rope_tables(pos_s), tm=N * T, dilated=False)
    oa = _samp_a(qa.reshape(N, T, QA_W), ka.reshape(N, T, W_A), va.reshape(N, T, W_A),
                 cache_a_k[l].reshape(N, -1, HEAD_DIM_A),
                 cache_a_v[l].reshape(N, -1, HEAD_DIM_A)).reshape(N * T, W_A)
    ob = _samp_b(sink, qb.reshape(N, T, QB_W), kvb.reshape(N, T, KVB_W),
                 cache_b_k[l].reshape(N, -1, LANES), cache_b_v[l].reshape(N, -1, LANES)).reshape(N * T, QB_W)
    oc = _xattn(qc.reshape(N, T, QC_W), cache_mem_k[l].reshape(N, -1, HEAD_DIM_C),
                cache_mem_v[l].reshape(N, -1, HEAD_DIM_C), nb=8, tq=T, interleaved=True).reshape(N * T, QC_W)
    y_sample = _layer_tail(xs, [oa], ob, oc, gt, wts).reshape(N, T, D)

    ka4 = ka.reshape(N, T, A_KV_HEADS, HEAD_DIM_A)
    va4 = va.reshape(N, T, A_KV_HEADS, HEAD_DIM_A)
    kvb4 = kvb.reshape(N, T, 2, B_KV_HEADS, HEAD_DIM_B)
    new_s = (ka4[None], va4[None], kvb4[:, :, 0][None], kvb4[:, :, 1][None])

    return (y_prompt, y_sample) + new_p + new_s
```
